```python
import math
import jax, jax.numpy as jnp
from jax import lax
import numpy as np

D_MODEL = 4096
BATCH = 2
SEQ = 8192
DEPTH = 2

CHUNK = 64
LEFT_CHUNKS = 8
REL_CLIP = 256
Q_BLOCK = 128
HEAD_DIM = 128
MIX_WIDTH = D_MODEL
A_HEADS = MIX_WIDTH // 4 // HEAD_DIM
B_HEADS = MIX_WIDTH // 4 // HEAD_DIM
A_WIDTH = A_HEADS * HEAD_DIM
B_WIDTH = B_HEADS * HEAD_DIM
SSM_WIDTH = MIX_WIDTH - A_WIDTH - B_WIDTH
SSM_GROUP = 16
SSM_GROUPS = SSM_WIDTH // SSM_GROUP
SSM_STATE = 64
IN_COLS = 3 * A_WIDTH + 3 * B_WIDTH + B_HEADS + SSM_WIDTH
D_FF = -(-8 * D_MODEL // (3 * 256)) * 256
RMS_EPS = 1e-6

kernel_name = "hybrid_chunk_causal_encoder"


def rms_norm(x, g):
    xf = x.astype(jnp.float32)
    y = xf * lax.rsqrt(jnp.mean(xf * xf, axis=-1, keepdims=True) + RMS_EPS)
    return (y * g.astype(jnp.float32)).astype(x.dtype)


def split_heads(t, n_heads):
    b, s, _ = t.shape
    return t.reshape(b, s, n_heads, HEAD_DIM).transpose(0, 2, 1, 3)


def merge_heads(t):
    b, h, s, d = t.shape
    return t.transpose(0, 2, 1, 3).reshape(b, s, h * d)


def chunk_band(t):
    b, h, s, d = t.shape
    n_c = s // CHUNK
    tc = t.reshape(b, h, n_c, CHUNK, d)
    tp = jnp.pad(tc, ((0, 0), (0, 0), (LEFT_CHUNKS, 0), (0, 0), (0, 0)))
    return jnp.concatenate([tp[:, :, j:j + n_c] for j in range(LEFT_CHUNKS + 1)], axis=3)


def chunked_relpos_attention(q, k, v, rel_bias):
    b, h, s, d = q.shape
    n_c = s // CHUNK
    band = (LEFT_CHUNKS + 1) * CHUNK
    qc = q.reshape(b, h, n_c, CHUNK, d)
    kb = chunk_band(k)
    vb = chunk_band(v)
    scores = jnp.einsum('bhcqd,bhckd->bhcqk', qc, kb).astype(jnp.float32) * (d ** -0.5)
    dist = LEFT_CHUNKS * CHUNK + jnp.arange(CHUNK)[:, None] - jnp.arange(band)[None, :]
    idx = jnp.clip(dist, -REL_CLIP, REL_CLIP) + REL_CLIP
    bias = rel_bias.astype(jnp.float32)[:, idx]
    valid = (jnp.arange(n_c)[:, None] - LEFT_CHUNKS + jnp.arange(band)[None, :] // CHUNK) >= 0
    scores = jnp.where(valid[None, None, :, None, :], scores + bias[None, :, None], -jnp.inf)
    p = jax.nn.softmax(scores, axis=-1)
    o = jnp.einsum('bhcqk,bhckd->bhcqd', p.astype(v.dtype), vb)
    return o.reshape(b, h, s, d)


def forgetting_attention(q, k, v, f_logit):
    b, h, s, d = q.shape
    n_q = s // Q_BLOCK
    cum = jnp.cumsum(jax.nn.log_sigmoid(f_logit.astype(jnp.float32)), axis=-1)
    q_blocks = jnp.moveaxis(q.reshape(b, h, n_q, Q_BLOCK, d), 2, 0)
    c_blocks = jnp.moveaxis(cum.reshape(b, h, n_q, Q_BLOCK), 2, 0)
    starts = jnp.arange(n_q, dtype=jnp.int32) * Q_BLOCK
    k_pos = jnp.arange(s, dtype=jnp.int32)
    scale = d ** -0.5

    def block(args):
        q_i, c_i, s0 = args
        logits = jnp.einsum('bhqd,bhkd->bhqk', q_i, k).astype(jnp.float32) * scale
        logits = logits + (c_i[..., None] - cum[:, :, None, :])
        q_pos = s0 + jnp.arange(Q_BLOCK, dtype=jnp.int32)
        logits = jnp.where(k_pos[None, :] <= q_pos[:, None], logits, -jnp.inf)
        p = jax.nn.softmax(logits, axis=-1)
        return jnp.einsum('bhqk,bhkd->bhqd', p.astype(v.dtype), v)

    o = lax.map(block, (q_blocks, c_blocks, starts))
    return jnp.moveaxis(o, 0, 2).reshape(b, h, s, d)


def s5_mixer(u, a_re, a_im, log_dt, b_re, b_im, c_re, c_im, d_skip, w_glu, b_glu):
    bsz, s, w = u.shape
    uf = u.astype(jnp.float32).reshape(bsz, s, SSM_GROUPS, SSM_GROUP)
    lam = lax.complex(jnp.minimum(a_re.astype(jnp.float32), -1e-4), a_im.astype(jnp.float32))
    dt = jnp.exp(log_dt.astype(jnp.float32))[:, None]
    lam_bar = jnp.exp(lam * dt)
    b_mat = lax.complex(b_re.astype(jnp.float32), b_im.astype(jnp.float32))
    b_bar = ((lam_bar - 1.0) / lam)[..., None] * b_mat
    bu = jnp.einsum('blgi,gni->blgn', uf.astype(jnp.complex64), b_bar)
    lam_seq = jnp.broadcast_to(lam_bar, bu.shape)

    def combine(e1, e2):
        a1, x1 = e1
        a2, x2 = e2
        return a1 * a2, a2 * x1 + x2

    _, states = lax.associative_scan(combine, (lam_seq, bu), axis=1)
    c_mat = lax.complex(c_re.astype(jnp.float32), c_im.astype(jnp.float32))
    y = jnp.einsum('blgn,gin->blgi', states, c_mat).real
    y = y + d_skip.astype(jnp.float32).reshape(SSM_GROUPS, SSM_GROUP) * uf
    y = jax.nn.gelu(y.reshape(bsz, s, w))
    y = y * jax.nn.sigmoid(y @ w_glu.astype(jnp.float32) + b_glu.astype(jnp.float32))
    return y.astype(u.dtype)


def setup_inputs(seed: int = 0) -> dict:
    key = jax.random.key(seed)
    ks = jax.random.split(key, 24)
    L = DEPTH
    f32 = jnp.float32
    nrm = lambda k, shape: jax.random.normal(k, shape, f32)
    return {
        "x": nrm(ks[0], (BATCH, SEQ, D_MODEL)),
        "w_in": nrm(ks[1], (L, D_MODEL, IN_COLS)) * D_MODEL ** -0.5,
        "w_out": nrm(ks[2], (L, MIX_WIDTH, D_MODEL)) * MIX_WIDTH ** -0.5,
        "g_pre_mix": 1.0 + 0.05 * nrm(ks[3], (L, D_MODEL)),
        "g_post_mix": 1.0 + 0.05 * nrm(ks[4], (L, D_MODEL)),
        "g_group": 1.0 + 0.05 * nrm(ks[5], (L, MIX_WIDTH)),
        "rel_bias": 0.1 * nrm(ks[6], (L, A_HEADS, 2 * REL_CLIP + 1)),
        "b_forget": 2.0 + 0.1 * nrm(ks[7], (L, B_HEADS)),
        "ssm_a_re": -0.5 + 0.01 * nrm(ks[8], (L, SSM_GROUPS, SSM_STATE)),
        "ssm_a_im": jnp.pi * jnp.arange(SSM_STATE, dtype=f32) + 0.01 * nrm(ks[9], (L, SSM_GROUPS, SSM_STATE)),
        "ssm_log_dt": jax.random.uniform(ks[10], (L, SSM_GROUPS), f32, minval=math.log(1e-3), maxval=math.log(1e-1)),
        "ssm_b_re": nrm(ks[11], (L, SSM_GROUPS, SSM_STATE, SSM_GROUP)) * (2 * SSM_GROUP) ** -0.5,
        "ssm_b_im": nrm(ks[12], (L, SSM_GROUPS, SSM_STATE, SSM_GROUP)) * (2 * SSM_GROUP) ** -0.5,
        "ssm_c_re": nrm(ks[13], (L, SSM_GROUPS, SSM_GROUP, SSM_STATE)) * SSM_STATE ** -0.5,
        "ssm_c_im": nrm(ks[14], (L, SSM_GROUPS, SSM_GROUP, SSM_STATE)) * SSM_STATE ** -0.5,
        "ssm_d": 0.5 * nrm(ks[15], (L, SSM_WIDTH)),
        "w_glu": nrm(ks[16], (L, SSM_WIDTH, SSM_WIDTH)) * SSM_WIDTH ** -0.5,
        "b_glu": 0.02 * nrm(ks[17], (L, SSM_WIDTH)),
        "g_pre_ffn": 1.0 + 0.05 * nrm(ks[18], (L, D_MODEL)),
        "g_post_ffn": 1.0 + 0.05 * nrm(ks[19], (L, D_MODEL)),
        "w_ffn_gate": nrm(ks[20], (L, D_MODEL, D_FF)) * D_MODEL ** -0.5,
        "w_ffn_up": nrm(ks[21], (L, D_MODEL, D_FF)) * D_MODEL ** -0.5,
        "w_ffn_down": nrm(ks[22], (L, D_FF, D_MODEL)) * D_FF ** -0.5,
    }


def reference(x, w_in, w_out, g_pre_mix, g_post_mix, g_group, rel_bias, b_forget,
              ssm_a_re, ssm_a_im, ssm_log_dt, ssm_b_re, ssm_b_im, ssm_c_re, ssm_c_im,
              ssm_d, w_glu, b_glu, g_pre_ffn, g_post_ffn, w_ffn_gate, w_ffn_up, w_ffn_down):
    splits = list(np.cumsum([A_WIDTH, A_WIDTH, A_WIDTH, B_WIDTH, B_WIDTH, B_WIDTH, B_HEADS]))
    for l in range(DEPTH):
        h = rms_norm(x, g_pre_mix[l])
        proj = h @ w_in[l]
        qa, ka, va, qb, kb, vb, fb, uc = jnp.split(proj, splits, axis=-1)
        o_a = merge_heads(chunked_relpos_attention(
            split_heads(qa, A_HEADS), split_heads(ka, A_HEADS), split_heads(va, A_HEADS), rel_bias[l]))
        f_logit = (fb + b_forget[l]).transpose(0, 2, 1)
        o_b = merge_heads(forgetting_attention(
            split_heads(qb, B_HEADS), split_heads(kb, B_HEADS), split_heads(vb, B_HEADS), f_logit))
        o_c = s5_mixer(uc, ssm_a_re[l], ssm_a_im[l], ssm_log_dt[l], ssm_b_re[l], ssm_b_im[l],
                       ssm_c_re[l], ssm_c_im[l], ssm_d[l], w_glu[l], b_glu[l])
        gg = g_group[l]
        mix = jnp.concatenate([
            rms_norm(o_a, gg[:A_WIDTH]),
            rms_norm(o_b, gg[A_WIDTH:A_WIDTH + B_WIDTH]),
            rms_norm(o_c, gg[A_WIDTH + B_WIDTH:]),
        ], axis=-1)
        x = x + rms_norm(mix @ w_out[l], g_post_mix[l])
        h = rms_norm(x, g_pre_ffn[l])
        f = (jax.nn.silu(h @ w_ffn_gate[l]) * (h @ w_ffn_up[l])) @ w_ffn_down[l]
        x = x + rms_norm(f, g_post_ffn[l])
    return x
```

```python
import functools
import math

import jax
import jax.numpy as jnp
from jax import lax
from jax.experimental import pallas as pl
from jax.experimental.pallas import tpu as pltpu

CHUNK = 64
LEFT_CHUNKS = 8
REL_CLIP = 256
HEAD_DIM = 128
SSM_GROUP = 16
SSM_STATE = 64
RMS_EPS = 1e-6

LANES = 128
VMEM_LIMIT_BYTES = 56 * 1024 * 1024

SSM_T = 128
A_PAIR = 2 * CHUNK
A_WIN = A_PAIR + LEFT_CHUNKS * CHUNK
NEG_BIG = -1e30

F32 = jnp.float32
BF16 = jnp.bfloat16


def _cparams(sem):
    return pltpu.CompilerParams(dimension_semantics=sem,
                                vmem_limit_bytes=VMEM_LIMIT_BYTES)


def _tile(n, want):
    t = min(n, want)
    while n % t:
        t //= 2
    return t


def _rms_rows(xf, g):
    return xf * lax.rsqrt(jnp.mean(xf * xf, axis=-1, keepdims=True) + RMS_EPS) * g


def _rmsnorm_kernel(x_ref, g_ref, o_ref):
    o_ref[...] = _rms_rows(x_ref[...], g_ref[...]).astype(o_ref.dtype)


def _rmsnorm(x, g, tm=256):
    n, d = x.shape
    tm = _tile(n, tm)
    return pl.pallas_call(
        _rmsnorm_kernel,
        out_shape=jax.ShapeDtypeStruct((n, d), BF16),
        grid=(n // tm,),
        in_specs=[pl.BlockSpec((tm, d), lambda i: (i, 0)),
                  pl.BlockSpec((1, d), lambda i: (0, 0))],
        out_specs=pl.BlockSpec((tm, d), lambda i: (i, 0)),
        compiler_params=_cparams(("parallel",)),
        name="rmsnorm",
    )(x, g.reshape(1, d))


def _mm_nn_kernel(a_ref, b_ref, o_ref):
    o_ref[...] = jnp.dot(a_ref[...], b_ref[...],
                         preferred_element_type=F32).astype(o_ref.dtype)


def _matmul_nn(a, b, tm=1024, tn=1024, name="matmul_nn"):
    m, k = a.shape
    _, n = b.shape
    tm, tn = _tile(m, tm), _tile(n, tn)
    return pl.pallas_call(
        _mm_nn_kernel,
        out_shape=jax.ShapeDtypeStruct((m, n), BF16),
        grid=(m // tm, n // tn),
        in_specs=[pl.BlockSpec((tm, k), lambda i, j: (i, 0)),
                  pl.BlockSpec((k, tn), lambda i, j: (0, j))],
        out_specs=pl.BlockSpec((tm, tn), lambda i, j: (i, j)),
        compiler_params=_cparams(("parallel", "parallel")),
        name=name,
    )(a, b)


def _mm_nt_kernel(a_ref, b_ref, o_ref):
    o_ref[...] = lax.dot_general(a_ref[...], b_ref[...],
                                 (((1,), (1,)), ((), ())),
                                 preferred_element_type=F32).astype(o_ref.dtype)


def _matmul_nt(a, b, tr=1024, tn=1024, name="matmul_nt"):
    r, k = a.shape
    n, _ = b.shape
    tr, tn = _tile(r, tr), _tile(n, tn)
    return pl.pallas_call(
        _mm_nt_kernel,
        out_shape=jax.ShapeDtypeStruct((r, n), BF16),
        grid=(n // tn, r // tr),
        in_specs=[pl.BlockSpec((tr, k), lambda i, j: (j, 0)),
                  pl.BlockSpec((tn, k), lambda i, j: (i, 0))],
        out_specs=pl.BlockSpec((tr, tn), lambda i, j: (j, i)),
        compiler_params=_cparams(("parallel", "parallel")),
        name=name,
    )(a, b)


def _fgate_kernel(h_ref, w_ref, b_ref, o_ref, carry_ref, *, tiles_per_seq):
    i = pl.program_id(0)

    @pl.when(i % tiles_per_seq == 0)
    def _():
        carry_ref[...] = jnp.zeros_like(carry_ref)

    logit = jnp.dot(h_ref[...], w_ref[...], preferred_element_type=F32) + b_ref[...]
    ls = -(jnp.maximum(-logit, 0.0) + jnp.log1p(jnp.exp(-jnp.abs(logit))))
    tm = ls.shape[0]
    row = lax.broadcasted_iota(jnp.int32, (tm, tm), 0)
    col = lax.broadcasted_iota(jnp.int32, (tm, tm), 1)
    tri = (col <= row).astype(F32)
    cs = jnp.dot(tri, ls, preferred_element_type=F32,
                 precision=lax.Precision.HIGHEST) + carry_ref[...]
    o_ref[...] = cs
    carry_ref[...] = cs[tm - 1:tm, :]


def _forget_cumsum(h, w_f, b_f, seq, tm=512):
    n, d = h.shape
    tm = _tile(seq, tm)
    return pl.pallas_call(
        functools.partial(_fgate_kernel, tiles_per_seq=seq // tm),
        out_shape=jax.ShapeDtypeStruct((n, LANES), F32),
        grid=(n // tm,),
        in_specs=[pl.BlockSpec((tm, d), lambda i: (i, 0)),
                  pl.BlockSpec((d, LANES), lambda i: (0, 0)),
                  pl.BlockSpec((1, LANES), lambda i: (0, 0))],
        out_specs=pl.BlockSpec((tm, LANES), lambda i: (i, 0)),
        scratch_shapes=[pltpu.VMEM((1, LANES), F32)],
        compiler_params=_cparams(("arbitrary",)),
        name="forget_cumsum",
    )(h, w_f, b_f)


def _attn_a_kernel(q_ref, kp_ref, kc_ref, vp_ref, vc_ref, bias_ref, o_ref, *, tq):
    qi = pl.program_id(2)
    n_pairs = tq // A_PAIR
    bias = bias_ref[0]
    col = lax.broadcasted_iota(jnp.int32, (A_PAIR, A_WIN), 1)
    for p in range(n_pairs):
        lo = p * A_PAIR
        ws = tq + lo - LEFT_CHUNKS * CHUNK
        q = q_ref[lo:lo + A_PAIR, :]
        k_parts, v_parts = [], []
        if ws < tq:
            k_parts.append(kp_ref[ws:tq, :])
            v_parts.append(vp_ref[ws:tq, :])
        k_parts.append(kc_ref[max(ws - tq, 0):lo + A_PAIR, :])
        v_parts.append(vc_ref[max(ws - tq, 0):lo + A_PAIR, :])
        k = jnp.concatenate(k_parts, axis=0) if len(k_parts) > 1 else k_parts[0]
        v = jnp.concatenate(v_parts, axis=0) if len(v_parts) > 1 else v_parts[0]
        s = lax.dot_general(q, k, (((1,), (1,)), ((), ())),
                            preferred_element_type=F32) + bias
        s = jnp.where(jnp.logical_or(qi > 0, col >= tq - ws), s, NEG_BIG)
        m = jnp.max(s, axis=-1, keepdims=True)
        e = jnp.exp(s - m)
        l = jnp.sum(e, axis=-1, keepdims=True)
        o = jnp.dot(e.astype(BF16), v, preferred_element_type=F32)
        o_ref[lo:lo + A_PAIR, :] = (o / l).astype(o_ref.dtype)


def _attention_a(proj, bias_tab, batch, seq, heads, q_col, k_col, v_col, tq=512):
    tq = _tile(seq, tq)
    assert tq >= LEFT_CHUNKS * CHUNK and tq % A_PAIR == 0
    nq = seq // tq

    def cur(col0):
        return pl.BlockSpec((tq, HEAD_DIM), lambda b, h, i: (b * nq + i, col0 + h))

    def prev(col0):
        return pl.BlockSpec((tq, HEAD_DIM),
                            lambda b, h, i: (b * nq + jnp.maximum(i - 1, 0), col0 + h))

    return pl.pallas_call(
        functools.partial(_attn_a_kernel, tq=tq),
        out_shape=jax.ShapeDtypeStruct((batch * seq, heads * HEAD_DIM), BF16),
        grid=(batch, heads, nq),
        in_specs=[cur(q_col), prev(k_col), cur(k_col), prev(v_col), cur(v_col),
                  pl.BlockSpec((1, A_PAIR, A_WIN), lambda b, h, i: (h, 0, 0))],
        out_specs=pl.BlockSpec((tq, HEAD_DIM), lambda b, h, i: (b * nq + i, h)),
        compiler_params=_cparams(("parallel", "parallel", "arbitrary")),
        name="attention_a",
    )(proj, proj, proj, proj, proj, bias_tab)


def _attn_a_bias_table(rel_bias):
    r = jnp.arange(A_PAIR)[:, None]
    c = jnp.arange(A_WIN)[None, :]
    dist = LEFT_CHUNKS * CHUNK + r - c
    idx = jnp.clip(dist, -REL_CLIP, REL_CLIP) + REL_CLIP
    lo = (r // CHUNK) * CHUNK
    band = jnp.logical_and(c >= lo, c < lo + (LEFT_CHUNKS + 1) * CHUNK)
    tab = rel_bias.astype(F32)[:, idx]
    return jnp.where(band[None], tab, NEG_BIG)


def _attn_b_kernel(q_ref, k_ref, v_ref, c_ref, o_ref, acc_ref, *, tq):
    qi = pl.program_id(2)
    q = q_ref[...]
    c_diag = c_ref[0, qi]
    c0 = c_diag[:, 0:1]

    def scores(ki, c_row):
        start = pl.multiple_of(ki * tq, tq)
        k = k_ref[pl.ds(start, tq), :]
        s = lax.dot_general(q, k, (((1,), (1,)), ((), ())),
                            preferred_element_type=F32)
        return s + (c0 - c_row), start

    s, start = scores(qi, c_diag)
    row = lax.broadcasted_iota(jnp.int32, (tq, tq), 0)
    col = lax.broadcasted_iota(jnp.int32, (tq, tq), 1)
    s = jnp.where(col <= row, s, NEG_BIG)
    m0 = jnp.max(s, axis=-1, keepdims=True)
    e = jnp.exp(s - m0)
    l0 = jnp.sum(e, axis=-1, keepdims=True)
    acc_ref[...] = jnp.dot(e.astype(BF16), v_ref[pl.ds(start, tq), :],
                           preferred_element_type=F32)

    def body(ki, carry):
        m, l = carry
        s, start = scores(ki, c_ref[0, ki])
        m_new = jnp.maximum(m, jnp.max(s, axis=-1, keepdims=True))
        e = jnp.exp(s - m_new)
        alpha = jnp.exp(m - m_new)
        l = alpha * l + jnp.sum(e, axis=-1, keepdims=True)
        acc_ref[...] = alpha * acc_ref[...] + jnp.dot(
            e.astype(BF16), v_ref[pl.ds(start, tq), :], preferred_element_type=F32)
        return m_new, l

    _, l = lax.fori_loop(0, qi, body, (m0, l0))
    o_ref[...] = (acc_ref[...] / l).astype(o_ref.dtype)


def _attention_b(proj, cum_t, batch, seq, heads, q_col, k_col, v_col, tq=512):
    tq = _tile(seq, tq)
    nq = seq // tq
    return pl.pallas_call(
        functools.partial(_attn_b_kernel, tq=tq),
        out_shape=jax.ShapeDtypeStruct((batch * seq, heads * HEAD_DIM), BF16),
        grid=(batch, heads, nq),
        in_specs=[pl.BlockSpec((tq, HEAD_DIM), lambda b, h, i: (b * nq + i, q_col + h)),
                  pl.BlockSpec((seq, HEAD_DIM), lambda b, h, i: (b, k_col + h)),
                  pl.BlockSpec((seq, HEAD_DIM), lambda b, h, i: (b, v_col + h)),
                  pl.BlockSpec((1, nq, 1, tq), lambda b, h, i: (b * heads + h, 0, 0, 0))],
        out_specs=pl.BlockSpec((tq, HEAD_DIM), lambda b, h, i: (b * nq + i, h)),
        scratch_shapes=[pltpu.VMEM((tq, HEAD_DIM), F32)],
        compiler_params=_cparams(("parallel", "parallel", "arbitrary")),
        name="attention_b",
    )(proj, proj, proj, cum_t)


def _ssm_prep_kernel(are_ref, aim_ref, ldt_ref, btr_ref, bti_ref, cr_ref, ci_ref,
                     kmat_ref, wst_ref, cpw_ref, a1_ref, a2_ref, ktab_ref):
    t_len, n_st, n_ch = SSM_T, SSM_STATE, SSM_GROUP
    lr = jnp.minimum(are_ref[0], -1e-4)
    li = aim_ref[0]
    dt = jnp.exp(ldt_ref[0])
    xr, xi = lr * dt, li * dt

    def power(tau):
        mag = jnp.exp(tau * xr)
        return mag * jnp.cos(tau * xi), mag * jnp.sin(tau * xi)

    one = jnp.ones((1, 1), F32)
    er, ei = power(one)
    den = lr * lr + li * li
    nr, ni = er - 1.0, ei
    cfr = (nr * lr + ni * li) / den
    cfi = (ni * lr - nr * li) / den
    btr, bti = btr_ref[0], bti_ref[0]
    bbr = cfr * btr - cfi * bti
    bbi = cfr * bti + cfi * btr
    cr, ci = cr_ref[0], ci_ref[0]

    tau = lax.broadcasted_iota(jnp.int32, (t_len, 1), 0).astype(F32)
    pr, pi = power(tau)
    qr, qi = power(tau + 1.0)
    rr, ri = power(float(t_len - 1) - tau)

    w_rows = []
    for ip in range(n_ch):
        c_r, c_i = cr[ip:ip + 1, :], ci[ip:ip + 1, :]
        w_rows.append(jnp.concatenate([c_r * bbr - c_i * bbi,
                                       -(c_r * bbi + c_i * bbr)], axis=1))
    w_cat = jnp.concatenate(w_rows, axis=0)
    p_cat = jnp.concatenate([pr, pi], axis=1)
    ktab_ref[...] = lax.dot_general(w_cat, p_cat, (((1,), (1,)), ((), ())),
                                    preferred_element_type=F32,
                                    precision=lax.Precision.HIGHEST)

    row = lax.broadcasted_iota(jnp.int32, (t_len, t_len), 0)
    col = lax.broadcasted_iota(jnp.int32, (t_len, t_len), 1)
    causal = col >= row

    def fill(i, carry):
        r0 = pl.multiple_of(i * t_len, t_len)
        for ip in range(n_ch):
            kvec = ktab_ref[pl.ds(ip * n_ch + i, 1), :]
            blk = pltpu.roll(jnp.broadcast_to(kvec, (t_len, t_len)), 0, 1,
                             stride=1, stride_axis=0)
            kmat_ref[0, pl.ds(r0, t_len), ip * t_len:(ip + 1) * t_len] = (
                jnp.where(causal, blk, 0.0).astype(kmat_ref.dtype))
        return carry

    lax.fori_loop(0, n_ch, fill, 0)

    for i in range(n_ch):
        br, bi = bbr[i:i + 1, :], bbi[i:i + 1, :]
        wst_ref[0, i * t_len:(i + 1) * t_len, :] = jnp.concatenate(
            [rr * br - ri * bi, rr * bi + ri * br], axis=1).astype(wst_ref.dtype)
        c_r, c_i = cr[i:i + 1, :], ci[i:i + 1, :]
        cpw_ref[0, i * t_len:(i + 1) * t_len, :] = jnp.concatenate(
            [c_r * qr - c_i * qi, -(c_r * qi + c_i * qr)], axis=1).astype(cpw_ref.dtype)

    kk = lax.broadcasted_iota(jnp.int32, (8, 1), 0)
    ar, ai = power(float(t_len) * jnp.exp2(kk.astype(F32)))
    a1_ref[0] = jnp.concatenate([ar, ar], axis=1)
    a2_ref[0] = jnp.concatenate([-ai, ai], axis=1)


def _ssm_prep(a_re, a_im, log_dt, b_re, b_im, c_re, c_im):
    g, n_st = a_re.shape
    n_ch, t_len = SSM_GROUP, SSM_T
    w = n_ch * t_len
    vec = lambda a: a.reshape(g, 1, -1).astype(F32)
    bt = lambda a: jnp.swapaxes(a, 1, 2).astype(F32)
    spec3 = lambda s1, s2: pl.BlockSpec((1, s1, s2), lambda i: (i, 0, 0))
    return pl.pallas_call(
        _ssm_prep_kernel,
        out_shape=(jax.ShapeDtypeStruct((g, w, w), BF16),
                   jax.ShapeDtypeStruct((g, w, 2 * n_st), BF16),
                   jax.ShapeDtypeStruct((g, w, 2 * n_st), BF16),
                   jax.ShapeDtypeStruct((g, 8, 2 * n_st), F32),
                   jax.ShapeDtypeStruct((g, 8, 2 * n_st), F32)),
        grid=(g,),
        in_specs=[spec3(1, n_st), spec3(1, n_st), spec3(1, 1),
                  spec3(n_ch, n_st), spec3(n_ch, n_st),
                  spec3(n_ch, n_st), spec3(n_ch, n_st)],
        out_specs=(spec3(w, w), spec3(w, 2 * n_st), spec3(w, 2 * n_st),
                   spec3(8, 2 * n_st), spec3(8, 2 * n_st)),
        scratch_shapes=[pltpu.VMEM((n_ch * n_ch, t_len), F32)],
        compiler_params=_cparams(("parallel",)),
        name="ssm_prep",
    )(vec(a_re), vec(a_im), log_dt.reshape(g, 1, 1).astype(F32),
      bt(b_re), bt(b_im), c_re.astype(F32), c_im.astype(F32))


def _gelu_tanh(x):
    return 0.5 * x * (1.0 + jnp.tanh(math.sqrt(2.0 / math.pi) * (x + 0.044715 * x * x * x)))


def _ssm_kernel(d_ref, u_ref, kmat_ref, wst_ref, cpw_ref, a1_ref, a2_ref, o_ref,
                *, chunks_per_seq):
    g = pl.program_id(0)
    n_ch, t_len, n2 = SSM_GROUP, SSM_T, 2 * SSM_STATE
    x = jnp.concatenate([u_ref[i] for i in range(n_ch)], axis=1)
    n_rows = x.shape[0]
    y = jnp.dot(x, kmat_ref[0], preferred_element_type=F32)
    e = jnp.dot(x, wst_ref[0], preferred_element_type=F32)

    pos = lax.broadcasted_iota(jnp.int32, (n_rows, n2), 0) % chunks_per_seq
    a1, a2 = a1_ref[0], a2_ref[0]
    step, k = 1, 0
    while step < chunks_per_seq:
        sh = jnp.where(pos >= step, pltpu.roll(e, step, 0), 0.0)
        e = e + a1[k:k + 1, :] * sh + a2[k:k + 1, :] * pltpu.roll(sh, SSM_STATE, 1)
        step, k = step * 2, k + 1
    s_start = jnp.where(pos >= 1, pltpu.roll(e, 1, 0), 0.0)
    y = y + lax.dot_general(s_start.astype(BF16), cpw_ref[0],
                            (((1,), (1,)), ((), ())), preferred_element_type=F32)
    for i in range(n_ch):
        yi = y[:, i * t_len:(i + 1) * t_len] + d_ref[g * n_ch + i] * u_ref[i].astype(F32)
        o_ref[i] = _gelu_tanh(yi).astype(o_ref.dtype)


def _ssm(u_t, d_skip, kmat, wst, cpw, a1, a2, seq):
    w, n = u_t.shape
    g = w // SSM_GROUP
    n_ch, t_len = SSM_GROUP, SSM_T
    c = n // t_len
    assert c // (n // seq) <= 2 ** 8
    gw = n_ch * t_len
    u3 = u_t.reshape(w, c, t_len)
    spec3 = lambda s1, s2: pl.BlockSpec((1, s1, s2), lambda i: (i, 0, 0))
    out = pl.pallas_call(
        functools.partial(_ssm_kernel, chunks_per_seq=seq // t_len),
        out_shape=jax.ShapeDtypeStruct((w, c, t_len), BF16),
        grid=(g,),
        in_specs=[pl.BlockSpec(memory_space=pltpu.SMEM),
                  pl.BlockSpec((n_ch, c, t_len), lambda i: (i, 0, 0)),
                  spec3(gw, gw), spec3(gw, 2 * SSM_STATE), spec3(gw, 2 * SSM_STATE),
                  spec3(8, 2 * SSM_STATE), spec3(8, 2 * SSM_STATE)],
        out_specs=pl.BlockSpec((n_ch, c, t_len), lambda i: (i, 0, 0)),
        compiler_params=_cparams(("parallel",)),
        name="ssm",
    )(d_skip.astype(F32), u3, kmat, wst, cpw, a1, a2)
    return out.reshape(w, n)


def _glu_kernel(w_ref, y_ref, b_ref, o_ref, *, tr):
    j = pl.program_id(1)
    z = jnp.dot(w_ref[...], y_ref[...], preferred_element_type=F32) + b_ref[...]
    rows = pl.ds(pl.multiple_of(j * tr, tr), tr)
    o_ref[...] = (y_ref[rows, :].astype(F32) * jax.nn.sigmoid(z)).astype(o_ref.dtype)


def _glu(y_t, w_glu_t, b_glu, tr=1024, tn=1024):
    w, n = y_t.shape
    tr, tn = _tile(w, tr), _tile(n, tn)
    return pl.pallas_call(
        functools.partial(_glu_kernel, tr=tr),
        out_shape=jax.ShapeDtypeStruct((w, n), BF16),
        grid=(n // tn, w // tr),
        in_specs=[pl.BlockSpec((tr, w), lambda i, j: (j, 0)),
                  pl.BlockSpec((w, tn), lambda i, j: (0, i)),
                  pl.BlockSpec((tr, 1), lambda i, j: (j, 0))],
        out_specs=pl.BlockSpec((tr, tn), lambda i, j: (j, i)),
        compiler_params=_cparams(("parallel", "arbitrary")),
        name="glu",
    )(w_glu_t, y_t, b_glu.reshape(w, 1).astype(F32))


def _outproj_kernel(oa_ref, ob_ref, oc_ref, ga_ref, gb_ref, gc_ref,
                    wa_ref, wb_ref, wc_ref, o_ref, na_ref, nb_ref, nc_ref):
    @pl.when(pl.program_id(1) == 0)
    def _():
        na_ref[...] = _rms_rows(oa_ref[...].astype(F32), ga_ref[...]).astype(BF16)
        nb_ref[...] = _rms_rows(ob_ref[...].astype(F32), gb_ref[...]).astype(BF16)
        oc = oc_ref[...].astype(F32)
        inv = lax.rsqrt(jnp.mean(oc * oc, axis=0, keepdims=True) + RMS_EPS)
        nc_ref[...] = (oc * inv * gc_ref[...]).astype(BF16)

    acc = jnp.dot(na_ref[...], wa_ref[...], preferred_element_type=F32)
    acc += jnp.dot(nb_ref[...], wb_ref[...], preferred_element_type=F32)
    acc += lax.dot_general(nc_ref[...], wc_ref[...], (((0,), (0,)), ((), ())),
                           preferred_element_type=F32)
    o_ref[...] = acc.astype(o_ref.dtype)


def _outproj(o_a, o_b, o_c_t, g_group, w_out, tm=512, tn=1024):
    n, wa = o_a.shape
    wb = o_b.shape[1]
    wc = o_c_t.shape[0]
    d = w_out.shape[1]
    tm, tn = _tile(n, tm), _tile(d, tn)
    g = g_group.astype(F32)
    return pl.pallas_call(
        _outproj_kernel,
        out_shape=jax.ShapeDtypeStruct((n, d), BF16),
        grid=(n // tm, d // tn),
        in_specs=[pl.BlockSpec((tm, wa), lambda i, j: (i, 0)),
                  pl.BlockSpec((tm, wb), lambda i, j: (i, 0)),
                  pl.BlockSpec((wc, tm), lambda i, j: (0, i)),
                  pl.BlockSpec((1, wa), lambda i, j: (0, 0)),
                  pl.BlockSpec((1, wb), lambda i, j: (0, 0)),
                  pl.BlockSpec((wc, 1), lambda i, j: (0, 0)),
                  pl.BlockSpec((wa, tn), lambda i, j: (0, j)),
                  pl.BlockSpec((wb, tn), lambda i, j: (0, j)),
                  pl.BlockSpec((wc, tn), lambda i, j: (0, j))],
        out_specs=pl.BlockSpec((tm, tn), lambda i, j: (i, j)),
        scratch_shapes=[pltpu.VMEM((tm, wa), BF16), pltpu.VMEM((tm, wb), BF16),
                        pltpu.VMEM((wc, tm), BF16)],
        compiler_params=_cparams(("parallel", "arbitrary")),
        name="outproj",
    )(o_a, o_b, o_c_t, g[:wa].reshape(1, wa), g[wa:wa + wb].reshape(1, wb),
      g[wa + wb:].reshape(wc, 1), w_out[:wa], w_out[wa:wa + wb], w_out[wa + wb:])


def _residual_kernel(y_ref, x_ref, gp_ref, gn_ref, xo_ref, ho_ref):
    xn = x_ref[...] + _rms_rows(y_ref[...].astype(F32), gp_ref[...])
    xo_ref[...] = xn
    ho_ref[...] = _rms_rows(xn, gn_ref[...]).astype(ho_ref.dtype)


def _residual_last_kernel(y_ref, x_ref, gp_ref, xo_ref):
    xo_ref[...] = x_ref[...] + _rms_rows(y_ref[...].astype(F32), gp_ref[...])


def _residual_norm(y, x, g_post, g_next, tm=256):
    n, d = x.shape
    tm = _tile(n, tm)
    row = pl.BlockSpec((tm, d), lambda i: (i, 0))
    vec = pl.BlockSpec((1, d), lambda i: (0, 0))
    gp = g_post.reshape(1, d).astype(F32)
    if g_next is None:
        return pl.pallas_call(
            _residual_last_kernel,
            out_shape=jax.ShapeDtypeStruct((n, d), F32),
            grid=(n // tm,), in_specs=[row, row, vec], out_specs=row,
            compiler_params=_cparams(("parallel",)), name="residual_last",
        )(y, x, gp), None
    return pl.pallas_call(
        _residual_kernel,
        out_shape=(jax.ShapeDtypeStruct((n, d), F32), jax.ShapeDtypeStruct((n, d), BF16)),
        grid=(n // tm,), in_specs=[row, row, vec, vec], out_specs=(row, row),
        compiler_params=_cparams(("parallel",)), name="residual_norm",
    )(y, x, gp, g_next.reshape(1, d).astype(F32))


def _ffn_kernel(h_ref, wg_ref, wu_ref, wd_ref, o_ref, acc_ref):
    f = pl.program_id(1)

    @pl.when(f == 0)
    def _():
        acc_ref[...] = jnp.zeros_like(acc_ref)

    h = h_ref[...]
    gate = jnp.dot(h, wg_ref[...], preferred_element_type=F32)
    up = jnp.dot(h, wu_ref[...], preferred_element_type=F32)
    act = (gate * jax.nn.sigmoid(gate) * up).astype(BF16)
    acc_ref[...] += jnp.dot(act, wd_ref[...], preferred_element_type=F32)

    @pl.when(f == pl.num_programs(1) - 1)
    def _():
        o_ref[...] = acc_ref[...].astype(o_ref.dtype)


def _ffn(h, w_gate, w_up, w_down, tm=512, tf=256):
    n, d = h.shape
    d_ff = w_gate.shape[1]
    tm = _tile(n, tm)
    assert d_ff % tf == 0
    return pl.pallas_call(
        _ffn_kernel,
        out_shape=jax.ShapeDtypeStruct((n, d), BF16),
        grid=(n // tm, d_ff // tf),
        in_specs=[pl.BlockSpec((tm, d), lambda i, f: (i, 0)),
                  pl.BlockSpec((d, tf), lambda i, f: (0, f)),
                  pl.BlockSpec((d, tf), lambda i, f: (0, f)),
                  pl.BlockSpec((tf, d), lambda i, f: (f, 0))],
        out_specs=pl.BlockSpec((tm, d), lambda i, f: (i, 0)),
        scratch_shapes=[pltpu.VMEM((tm, d), F32)],
        compiler_params=_cparams(("parallel", "arbitrary")),
        name="ffn",
    )(h, w_gate, w_up, w_down)


def kernel(x, w_in, w_out, g_pre_mix, g_post_mix, g_group, rel_bias, b_forget,
           ssm_a_re, ssm_a_im, ssm_log_dt, ssm_b_re, ssm_b_im, ssm_c_re, ssm_c_im,
           ssm_d, w_glu, b_glu, g_pre_ffn, g_post_ffn, w_ffn_gate, w_ffn_up, w_ffn_down):
    batch, seq, d_model = x.shape
    depth = w_in.shape[0]
    a_heads = rel_bias.shape[1]
    b_heads = b_forget.shape[1]
    a_w, b_w = a_heads * HEAD_DIM, b_heads * HEAD_DIM
    ssm_w = ssm_d.shape[1]
    n_tok = batch * seq
    tok_cols = 3 * a_w + 3 * b_w
    assert w_in.shape[2] == tok_cols + b_heads + ssm_w
    assert seq % SSM_T == 0 and b_heads <= LANES
    scale = HEAD_DIM ** -0.5

    xf = x.reshape(n_tok, d_model).astype(F32)
    h = _rmsnorm(xf, g_pre_mix[0].astype(F32))
    for l in range(depth):
        wl = w_in[l].astype(F32)
        col_scale = jnp.concatenate([
            jnp.full((a_w,), scale, F32), jnp.ones((2 * a_w,), F32),
            jnp.full((b_w,), scale, F32), jnp.ones((2 * b_w,), F32)])
        w_tok = (wl[:, :tok_cols] * col_scale).astype(BF16)
        w_f = jnp.pad(wl[:, tok_cols:tok_cols + b_heads],
                      ((0, 0), (0, LANES - b_heads))).astype(BF16)
        b_f = jnp.pad(b_forget[l].astype(F32), (0, LANES - b_heads)).reshape(1, LANES)
        w_uc_t = wl[:, tok_cols + b_heads:].T.astype(BF16)

        proj = _matmul_nn(h, w_tok, name="inproj_tok")
        u_t = _matmul_nt(w_uc_t, h, name="inproj_ssm")
        cum = _forget_cumsum(h, w_f, b_f, seq)

        bias_tab = _attn_a_bias_table(rel_bias[l])
        o_a = _attention_a(proj, bias_tab, batch, seq, a_heads,
                           0, a_heads, 2 * a_heads)
        tq_b = _tile(seq, 512)
        c0 = 3 * a_heads
        cum_t = cum[:, :b_heads].reshape(batch, seq, b_heads).transpose(0, 2, 1)
        cum_t = cum_t.reshape(batch * b_heads, seq // tq_b, 1, tq_b)
        o_b = _attention_b(proj, cum_t, batch, seq, b_heads,
                           c0, c0 + b_heads, c0 + 2 * b_heads, tq=tq_b)
        kmat, wst, cpw, a1, a2 = _ssm_prep(ssm_a_re[l], ssm_a_im[l], ssm_log_dt[l],
                                           ssm_b_re[l], ssm_b_im[l], ssm_c_re[l], ssm_c_im[l])
        y_t = _ssm(u_t, ssm_d[l], kmat, wst, cpw, a1, a2, seq)
        o_c_t = _glu(y_t, w_glu[l].astype(F32).T.astype(BF16), b_glu[l])

        mix = _outproj(o_a, o_b, o_c_t, g_group[l], w_out[l].astype(BF16))
        xf, h2 = _residual_norm(mix, xf, g_post_mix[l], g_pre_ffn[l])

        f = _ffn(h2, w_ffn_gate[l].astype(BF16), w_ffn_up[l].astype(BF16),
                 w_ffn_down[l].astype(BF16))
        g_next = g_pre_mix[l + 1] if l + 1 < depth else None
        xf, h = _residual_norm(f, xf, g_post_ffn[l], g_next)
    return xf.reshape(batch, seq, d_model).astype(x.dtype)
```

```python
import functools
import math

import jax
import jax.numpy as jnp
from jax import lax
from jax.experimental import pallas as pl
from jax.experimental.pallas import tpu as pltpu

CHUNK = 64
LEFT_CHUNKS = 8
REL_CLIP = 256
HEAD_DIM = 128
SSM_GROUP = 16
SSM_STATE = 64
RMS_EPS = 1e-6

LANES = 128
VMEM_LIMIT_BYTES = 56 * 1024 * 1024

SSM_T = 128
A_PAIR = 2 * CHUNK
A_WIN = A_PAIR + LEFT_CHUNKS * CHUNK
NEG_BIG = -1e30
LOG2E = math.log2(math.e)
B_AUG = 3

F32 = jnp.float32
BF16 = jnp.bfloat16


def _cparams(sem):
    return pltpu.CompilerParams(dimension_semantics=sem,
                                vmem_limit_bytes=VMEM_LIMIT_BYTES)


def _tile(n, want):
    t = min(n, want)
    while n % t:
        t //= 2
    return t


def _rms_rows(xf, g):
    return xf * lax.rsqrt(jnp.mean(xf * xf, axis=-1, keepdims=True) + RMS_EPS) * g


def _rmsnorm_kernel(x_ref, g_ref, o_ref):
    o_ref[...] = _rms_rows(x_ref[...], g_ref[...]).astype(o_ref.dtype)


def _rmsnorm(x, g, tm=256):
    n, d = x.shape
    tm = _tile(n, tm)
    return pl.pallas_call(
        _rmsnorm_kernel,
        out_shape=jax.ShapeDtypeStruct((n, d), BF16),
        grid=(n // tm,),
        in_specs=[pl.BlockSpec((tm, d), lambda i: (i, 0)),
                  pl.BlockSpec((1, d), lambda i: (0, 0))],
        out_specs=pl.BlockSpec((tm, d), lambda i: (i, 0)),
        compiler_params=_cparams(("parallel",)),
        name="rmsnorm",
    )(x, g.reshape(1, d))


def _inproj_tok_kernel(a_ref, w_ref, s_ref, o_ref, wb_ref):
    @pl.when(pl.program_id(1) == 0)
    def _():
        wb_ref[...] = (w_ref[...] * s_ref[...]).astype(BF16)

    o_ref[...] = jnp.dot(a_ref[...], wb_ref[...],
                         preferred_element_type=F32).astype(o_ref.dtype)


def _inproj_tok(a, w, col_scale, n_cols, tm=1024, tn=512):
    m, k = a.shape
    tm, tn = _tile(m, tm), _tile(n_cols, tn)
    return pl.pallas_call(
        _inproj_tok_kernel,
        out_shape=jax.ShapeDtypeStruct((m, n_cols), BF16),
        grid=(n_cols // tn, m // tm),
        in_specs=[pl.BlockSpec((tm, k), lambda j, i: (i, 0)),
                  pl.BlockSpec((k, tn), lambda j, i: (0, j)),
                  pl.BlockSpec((1, tn), lambda j, i: (0, j))],
        out_specs=pl.BlockSpec((tm, tn), lambda j, i: (i, j)),
        scratch_shapes=[pltpu.VMEM((k, tn), BF16)],
        compiler_params=_cparams(("parallel", "arbitrary")),
        name="inproj_tok",
    )(a, w, col_scale.reshape(1, n_cols))


def _mm_nt_kernel(a_ref, b_ref, o_ref):
    o_ref[...] = lax.dot_general(a_ref[...], b_ref[...],
                                 (((1,), (1,)), ((), ())),
                                 preferred_element_type=F32).astype(o_ref.dtype)


def _matmul_nt(a, b, tr=1024, tn=1024, name="matmul_nt"):
    r, k = a.shape
    n, _ = b.shape
    tr, tn = _tile(r, tr), _tile(n, tn)
    return pl.pallas_call(
        _mm_nt_kernel,
        out_shape=jax.ShapeDtypeStruct((r, n), BF16),
        grid=(n // tn, r // tr),
        in_specs=[pl.BlockSpec((tr, k), lambda i, j: (j, 0)),
                  pl.BlockSpec((tn, k), lambda i, j: (i, 0))],
        out_specs=pl.BlockSpec((tr, tn), lambda i, j: (j, i)),
        compiler_params=_cparams(("parallel", "parallel")),
        name=name,
    )(a, b)


def _fgate_kernel(h_ref, w_ref, b_ref, o_ref, carry_ref, *, tiles_per_seq, heads):
    i = pl.program_id(0)

    @pl.when(i % tiles_per_seq == 0)
    def _():
        carry_ref[...] = jnp.zeros_like(carry_ref)

    logit = jnp.dot(h_ref[...], w_ref[...], preferred_element_type=F32) + b_ref[...]
    ls = -(jnp.maximum(-logit, 0.0) + jnp.log1p(jnp.exp(-jnp.abs(logit))))
    tm = ls.shape[0]
    row = lax.broadcasted_iota(jnp.int32, (tm, tm), 0)
    col = lax.broadcasted_iota(jnp.int32, (tm, tm), 1)
    tri = (col <= row).astype(F32)
    cs = jnp.dot(tri, ls, preferred_element_type=F32,
                 precision=lax.Precision.HIGHEST) + carry_ref[...]
    carry_ref[...] = cs[tm - 1:tm, :]
    lane = lax.broadcasted_iota(jnp.int32, (tm, LANES), 1)
    for hd in range(heads):
        x = cs[:, hd:hd + 1] * (-LOG2E)
        hi = x.astype(BF16).astype(F32)
        mid = (x - hi).astype(BF16).astype(F32)
        lo = x - hi - mid
        o_ref[hd] = jnp.where(lane == 0, hi, jnp.where(lane == 1, mid, jnp.where(
            lane == 2, lo, 0.0))).astype(o_ref.dtype)


def _forget_bias(h, w_f, b_f, batch, seq, heads, tm=512):
    n, d = h.shape
    tm = _tile(seq, tm)
    tps = seq // tm
    return pl.pallas_call(
        functools.partial(_fgate_kernel, tiles_per_seq=tps, heads=heads),
        out_shape=jax.ShapeDtypeStruct((batch * heads, seq, LANES), BF16),
        grid=(n // tm,),
        in_specs=[pl.BlockSpec((tm, d), lambda i: (i, 0)),
                  pl.BlockSpec((d, LANES), lambda i: (0, 0)),
                  pl.BlockSpec((1, LANES), lambda i: (0, 0))],
        out_specs=pl.BlockSpec((heads, tm, LANES), lambda i: (i // tps, i % tps, 0)),
        scratch_shapes=[pltpu.VMEM((1, LANES), F32)],
        compiler_params=_cparams(("arbitrary",)),
        name="forget_bias",
    )(h, w_f, b_f)


def _attn_a_kernel(q_ref, kp_ref, kc_ref, vp_ref, vc_ref, ext_ref, o_ref, bias_ref, *, tq):
    qi = pl.program_id(2)
    n_pairs = tq // A_PAIR

    @pl.when(qi == 0)
    def _():
        r = lax.broadcasted_iota(jnp.int32, (A_PAIR, A_PAIR), 0)
        c = lax.broadcasted_iota(jnp.int32, (A_PAIR, A_PAIR), 1)
        n_seg = A_WIN // A_PAIR + 1
        rolled = [pltpu.roll(jnp.broadcast_to(
            ext_ref[0, :, j * A_PAIR:(j + 1) * A_PAIR], (A_PAIR, A_PAIR)), 0, 1,
            stride=1, stride_axis=0) for j in range(n_seg)]
        lo = (r // CHUNK) * CHUNK
        for j in range(n_seg - 1):
            tile = jnp.where(c >= r, rolled[j + 1], rolled[j])
            cc = c + j * A_PAIR
            band = jnp.logical_and(cc >= lo, cc < lo + (LEFT_CHUNKS + 1) * CHUNK)
            bias_ref[:, j * A_PAIR:(j + 1) * A_PAIR] = jnp.where(band, tile, NEG_BIG)

    bias = bias_ref[...]
    col = lax.broadcasted_iota(jnp.int32, (A_PAIR, A_WIN), 1)
    for p in range(n_pairs):
        lo = p * A_PAIR
        ws = tq + lo - LEFT_CHUNKS * CHUNK
        q = q_ref[lo:lo + A_PAIR, :]
        k_parts, v_parts = [], []
        if ws < tq:
            k_parts.append(kp_ref[ws:tq, :])
            v_parts.append(vp_ref[ws:tq, :])
        k_parts.append(kc_ref[max(ws - tq, 0):lo + A_PAIR, :])
        v_parts.append(vc_ref[max(ws - tq, 0):lo + A_PAIR, :])
        k = jnp.concatenate(k_parts, axis=0) if len(k_parts) > 1 else k_parts[0]
        v = jnp.concatenate(v_parts, axis=0) if len(v_parts) > 1 else v_parts[0]
        s = lax.dot_general(q, k, (((1,), (1,)), ((), ())),
                            preferred_element_type=F32) + bias
        s = jnp.where(jnp.logical_or(qi > 0, col >= tq - ws), s, NEG_BIG)
        m = jnp.max(s, axis=-1, keepdims=True)
        e = jnp.exp(s - m)
        l = jnp.sum(e, axis=-1, keepdims=True)
        o = jnp.dot(e.astype(BF16), v, preferred_element_type=F32)
        o_ref[lo:lo + A_PAIR, :] = (o / l).astype(o_ref.dtype)


def _attention_a(proj, bias_ext, batch, seq, heads, q_col, k_col, v_col, tq=512):
    tq = _tile(seq, tq)
    assert tq >= LEFT_CHUNKS * CHUNK and tq % A_PAIR == 0
    nq = seq // tq

    def cur(col0):
        return pl.BlockSpec((tq, HEAD_DIM), lambda b, h, i: (b * nq + i, col0 + h))

    def prev(col0):
        return pl.BlockSpec((tq, HEAD_DIM),
                            lambda b, h, i: (b * nq + jnp.maximum(i - 1, 0), col0 + h))

    return pl.pallas_call(
        functools.partial(_attn_a_kernel, tq=tq),
        out_shape=jax.ShapeDtypeStruct((batch * seq, heads * HEAD_DIM), BF16),
        grid=(batch, heads, nq),
        in_specs=[cur(q_col), prev(k_col), cur(k_col), prev(v_col), cur(v_col),
                  pl.BlockSpec((1, 1, A_WIN + A_PAIR), lambda b, h, i: (h, 0, 0))],
        out_specs=pl.BlockSpec((tq, HEAD_DIM), lambda b, h, i: (b * nq + i, h)),
        scratch_shapes=[pltpu.VMEM((A_PAIR, A_WIN), F32)],
        compiler_params=_cparams(("parallel", "parallel", "arbitrary")),
        name="attention_a",
    )(proj, proj, proj, proj, proj, bias_ext)


def _attn_a_bias_vector(rel_bias):
    x = jnp.arange(A_WIN + A_PAIR)
    idx = jnp.clip(LEFT_CHUNKS * CHUNK + A_PAIR - x, -REL_CLIP, REL_CLIP) + REL_CLIP
    return rel_bias.astype(F32)[:, idx].reshape(rel_bias.shape[0], 1, A_WIN + A_PAIR)


def _attn_b_kernel(q_ref, k_ref, v_ref, aug_ref, o_ref, acc_ref, s_ref, m_ref, *, tq, sub):
    qi = pl.program_id(2)
    n_sub = tq // sub
    nt = (((1,), (1,)), ((), ()))
    lane = lax.broadcasted_iota(jnp.int32, (tq, HEAD_DIM), 1)
    qx = jnp.concatenate([q_ref[...], jnp.where(lane < B_AUG, 1.0, 0.0).astype(BF16)],
                         axis=1)
    ones = jnp.ones((tq, HEAD_DIM), BF16)

    def key_block(ki):
        rows = pl.ds(pl.multiple_of(ki * tq, tq), tq)
        return jnp.concatenate([k_ref[rows, :], aug_ref[0, rows, :]], axis=1)

    def value_block(ki):
        rows = pl.ds(pl.multiple_of(ki * tq, tq), tq)
        return jnp.concatenate([v_ref[rows, :], ones], axis=1)

    def absorb(r, vx, n_keys, masked):
        rows = slice(r * sub, (r + 1) * sub)
        s = s_ref[rows, :n_keys]
        if masked:
            row = lax.broadcasted_iota(jnp.int32, (sub, n_keys), 0) + r * sub
            col = lax.broadcasted_iota(jnp.int32, (sub, n_keys), 1)
            s = jnp.where(col <= row, s, NEG_BIG)
        m_old = m_ref[rows, :]
        m_new = jnp.maximum(m_old, jnp.max(s, axis=-1, keepdims=True))
        e = jnp.exp2(s - m_new).astype(BF16)
        acc_ref[rows, :] = jnp.exp2(m_old - m_new) * acc_ref[rows, :] + jnp.dot(
            e, vx[:n_keys], preferred_element_type=F32)
        m_ref[rows, :] = m_new

    acc_ref[...] = jnp.zeros_like(acc_ref)
    m_ref[...] = jnp.full_like(m_ref, -jnp.inf)
    s_ref[...] = lax.dot_general(qx, key_block(0), nt, preferred_element_type=F32)

    def body(ki, carry):
        kx = key_block(ki + 1)
        vx = value_block(ki)
        for r in range(n_sub):
            rows = slice(r * sub, (r + 1) * sub)
            s_next = lax.dot_general(qx[rows], kx, nt, preferred_element_type=F32)
            absorb(r, vx, tq, False)
            s_ref[rows, :] = s_next
        return carry

    lax.fori_loop(0, qi, body, 0)
    vx = value_block(qi)
    for r in range(n_sub):
        absorb(r, vx, (r + 1) * sub, True)
    acc = acc_ref[...]
    o_ref[...] = (acc[:, :HEAD_DIM] / acc[:, HEAD_DIM:HEAD_DIM + 1]).astype(o_ref.dtype)


def _attention_b(proj, k_aug, batch, seq, heads, q_col, k_col, v_col, tq=1024, sub=256):
    tq = _tile(seq, tq)
    sub = _tile(tq, sub)
    nq = seq // tq
    return pl.pallas_call(
        functools.partial(_attn_b_kernel, tq=tq, sub=sub),
        out_shape=jax.ShapeDtypeStruct((batch * seq, heads * HEAD_DIM), BF16),
        grid=(batch, heads, nq),
        in_specs=[pl.BlockSpec((tq, HEAD_DIM), lambda b, h, i: (b * nq + i, q_col + h)),
                  pl.BlockSpec((seq, HEAD_DIM), lambda b, h, i: (b, k_col + h)),
                  pl.BlockSpec((seq, HEAD_DIM), lambda b, h, i: (b, v_col + h)),
                  pl.BlockSpec((1, seq, LANES), lambda b, h, i: (b * heads + h, 0, 0))],
        out_specs=pl.BlockSpec((tq, HEAD_DIM), lambda b, h, i: (b * nq + i, h)),
        scratch_shapes=[pltpu.VMEM((tq, 2 * HEAD_DIM), F32), pltpu.VMEM((tq, tq), F32),
                        pltpu.VMEM((tq, 1), F32)],
        compiler_params=_cparams(("parallel", "parallel", "arbitrary")),
        name="attention_b",
    )(proj, proj, proj, k_aug)


def _ssm_prep_kernel(are_ref, aim_ref, ldt_ref, btr_ref, bti_ref, cr_ref, ci_ref,
                     kmat_ref, wst_ref, cpw_ref, a1_ref, a2_ref, ktab_ref):
    t_len, n_st, n_ch = SSM_T, SSM_STATE, SSM_GROUP
    lr = jnp.minimum(are_ref[0], -1e-4)
    li = aim_ref[0]
    dt = jnp.exp(ldt_ref[0])
    xr, xi = lr * dt, li * dt

    def power(tau):
        mag = jnp.exp(tau * xr)
        return mag * jnp.cos(tau * xi), mag * jnp.sin(tau * xi)

    one = jnp.ones((1, 1), F32)
    er, ei = power(one)
    den = lr * lr + li * li
    nr, ni = er - 1.0, ei
    cfr = (nr * lr + ni * li) / den
    cfi = (ni * lr - nr * li) / den
    btr, bti = btr_ref[0], bti_ref[0]
    bbr = cfr * btr - cfi * bti
    bbi = cfr * bti + cfi * btr
    cr, ci = cr_ref[0], ci_ref[0]

    tau = lax.broadcasted_iota(jnp.int32, (t_len, 2 * n_st), 0).astype(F32)
    lane = lax.broadcasted_iota(jnp.int32, (t_len, 2 * n_st), 1)
    tau = jnp.where(lane < n_st, tau, float(t_len - 1) - tau)
    mag = jnp.exp(tau * jnp.concatenate([xr, xr], axis=1))
    ang = tau * jnp.concatenate([xi, xi], axis=1)
    pw_r, pw_i = mag * jnp.cos(ang), mag * jnp.sin(ang)
    pr, pi = pw_r[:, :n_st], pw_i[:, :n_st]
    rr, ri = pw_r[:, n_st:], pw_i[:, n_st:]
    qr, qi = pr * er - pi * ei, pr * ei + pi * er

    w_rows = []
    for ip in range(n_ch):
        c_r, c_i = cr[ip:ip + 1, :], ci[ip:ip + 1, :]
        w_rows.append(jnp.concatenate([c_r * bbr - c_i * bbi,
                                       -(c_r * bbi + c_i * bbr)], axis=1))
    w_cat = jnp.concatenate(w_rows, axis=0)
    p_cat = jnp.concatenate([pr, pi], axis=1)
    ktab_ref[...] = lax.dot_general(w_cat, p_cat, (((1,), (1,)), ((), ())),
                                    preferred_element_type=F32,
                                    precision=lax.Precision.HIGHEST)

    row = lax.broadcasted_iota(jnp.int32, (t_len, t_len), 0)
    col = lax.broadcasted_iota(jnp.int32, (t_len, t_len), 1)
    causal = col >= row

    def fill(i, carry):
        r0 = pl.multiple_of(i * t_len, t_len)
        for ip in range(n_ch):
            kvec = ktab_ref[pl.ds(ip * n_ch + i, 1), :]
            blk = pltpu.roll(jnp.broadcast_to(kvec, (t_len, t_len)), 0, 1,
                             stride=1, stride_axis=0)
            kmat_ref[0, pl.ds(r0, t_len), ip * t_len:(ip + 1) * t_len] = (
                jnp.where(causal, blk, 0.0).astype(kmat_ref.dtype))
        return carry

    lax.fori_loop(0, n_ch, fill, 0)

    for i in range(n_ch):
        br, bi = bbr[i:i + 1, :], bbi[i:i + 1, :]
        wst_ref[0, i * t_len:(i + 1) * t_len, :] = jnp.concatenate(
            [rr * br - ri * bi, rr * bi + ri * br], axis=1).astype(wst_ref.dtype)
        c_r, c_i = cr[i:i + 1, :], ci[i:i + 1, :]
        cpw_ref[0, i * t_len:(i + 1) * t_len, :] = jnp.concatenate(
            [c_r * qr - c_i * qi, -(c_r * qi + c_i * qr)], axis=1).astype(cpw_ref.dtype)

    kk = lax.broadcasted_iota(jnp.int32, (8, 1), 0)
    ar, ai = power(float(t_len) * jnp.exp2(kk.astype(F32)))
    a1_ref[0] = jnp.concatenate([ar, ar], axis=1)
    a2_ref[0] = jnp.concatenate([-ai, ai], axis=1)


def _ssm_prep(a_re, a_im, log_dt, b_re, b_im, c_re, c_im):
    g, n_st = a_re.shape
    n_ch, t_len = SSM_GROUP, SSM_T
    w = n_ch * t_len
    vec = lambda a: a.reshape(g, 1, -1).astype(F32)
    bt = lambda a: jnp.swapaxes(a, 1, 2).astype(F32)
    spec3 = lambda s1, s2: pl.BlockSpec((1, s1, s2), lambda i: (i, 0, 0))
    return pl.pallas_call(
        _ssm_prep_kernel,
        out_shape=(jax.ShapeDtypeStruct((g, w, w), BF16),
                   jax.ShapeDtypeStruct((g, w, 2 * n_st), BF16),
                   jax.ShapeDtypeStruct((g, w, 2 * n_st), BF16),
                   jax.ShapeDtypeStruct((g, 8, 2 * n_st), F32),
                   jax.ShapeDtypeStruct((g, 8, 2 * n_st), F32)),
        grid=(g,),
        in_specs=[spec3(1, n_st), spec3(1, n_st), spec3(1, 1),
                  spec3(n_ch, n_st), spec3(n_ch, n_st),
                  spec3(n_ch, n_st), spec3(n_ch, n_st)],
        out_specs=(spec3(w, w), spec3(w, 2 * n_st), spec3(w, 2 * n_st),
                   spec3(8, 2 * n_st), spec3(8, 2 * n_st)),
        scratch_shapes=[pltpu.VMEM((n_ch * n_ch, t_len), F32)],
        compiler_params=_cparams(("parallel",)),
        name="ssm_prep",
    )(vec(a_re), vec(a_im), log_dt.reshape(g, 1, 1).astype(F32),
      bt(b_re), bt(b_im), c_re.astype(F32), c_im.astype(F32))


def _gelu_tanh(x):
    return 0.5 * x * (1.0 + jnp.tanh(math.sqrt(2.0 / math.pi) * (x + 0.044715 * x * x * x)))


def _ssm_kernel(d_ref, u_ref, kmat_ref, wst_ref, cpw_ref, a1_ref, a2_ref, o_ref,
                *, chunks_per_seq):
    g = pl.program_id(0)
    n_ch, t_len, n2 = SSM_GROUP, SSM_T, 2 * SSM_STATE
    x = jnp.concatenate([u_ref[i] for i in range(n_ch)], axis=1)
    n_rows = x.shape[0]
    y = jnp.dot(x, kmat_ref[0], preferred_element_type=F32)
    e = jnp.dot(x, wst_ref[0], preferred_element_type=F32)

    pos = lax.broadcasted_iota(jnp.int32, (n_rows, n2), 0) % chunks_per_seq
    a1, a2 = a1_ref[0], a2_ref[0]
    step, k = 1, 0
    while step < chunks_per_seq:
        sh = jnp.where(pos >= step, pltpu.roll(e, step, 0), 0.0)
        e = e + a1[k:k + 1, :] * sh + a2[k:k + 1, :] * pltpu.roll(sh, SSM_STATE, 1)
        step, k = step * 2, k + 1
    s_start = jnp.where(pos >= 1, pltpu.roll(e, 1, 0), 0.0)
    y = y + lax.dot_general(s_start.astype(BF16), cpw_ref[0],
                            (((1,), (1,)), ((), ())), preferred_element_type=F32)
    for i in range(n_ch):
        yi = y[:, i * t_len:(i + 1) * t_len] + d_ref[g * n_ch + i] * u_ref[i].astype(F32)
        o_ref[i] = _gelu_tanh(yi).astype(o_ref.dtype)


def _ssm(u_t, d_skip, kmat, wst, cpw, a1, a2, seq):
    w, n = u_t.shape
    g = w // SSM_GROUP
    n_ch, t_len = SSM_GROUP, SSM_T
    c = n // t_len
    assert c // (n // seq) <= 2 ** 8
    gw = n_ch * t_len
    u3 = u_t.reshape(w, c, t_len)
    spec3 = lambda s1, s2: pl.BlockSpec((1, s1, s2), lambda i: (i, 0, 0))
    out = pl.pallas_call(
        functools.partial(_ssm_kernel, chunks_per_seq=seq // t_len),
        out_shape=jax.ShapeDtypeStruct((w, c, t_len), BF16),
        grid=(g,),
        in_specs=[pl.BlockSpec(memory_space=pltpu.SMEM),
                  pl.BlockSpec((n_ch, c, t_len), lambda i: (i, 0, 0)),
                  spec3(gw, gw), spec3(gw, 2 * SSM_STATE), spec3(gw, 2 * SSM_STATE),
                  spec3(8, 2 * SSM_STATE), spec3(8, 2 * SSM_STATE)],
        out_specs=pl.BlockSpec((n_ch, c, t_len), lambda i: (i, 0, 0)),
        compiler_params=_cparams(("parallel",)),
        name="ssm",
    )(d_skip.astype(F32), u3, kmat, wst, cpw, a1, a2)
    return out.reshape(w, n)


def _glu_kernel(w_ref, y_ref, b_ref, o_ref, *, tr):
    j = pl.program_id(1)
    z = jnp.dot(w_ref[...], y_ref[...], preferred_element_type=F32) + b_ref[...]
    rows = pl.ds(pl.multiple_of(j * tr, tr), tr)
    o_ref[...] = (y_ref[rows, :].astype(F32) * jax.nn.sigmoid(z)).astype(o_ref.dtype)


def _glu(y_t, w_glu_t, b_glu, tr=1024, tn=1024):
    w, n = y_t.shape
    tr, tn = _tile(w, tr), _tile(n, tn)
    return pl.pallas_call(
        functools.partial(_glu_kernel, tr=tr),
        out_shape=jax.ShapeDtypeStruct((w, n), BF16),
        grid=(n // tn, w // tr),
        in_specs=[pl.BlockSpec((tr, w), lambda i, j: (j, 0)),
                  pl.BlockSpec((w, tn), lambda i, j: (0, i)),
                  pl.BlockSpec((tr, 1), lambda i, j: (j, 0))],
        out_specs=pl.BlockSpec((tr, tn), lambda i, j: (j, i)),
        compiler_params=_cparams(("parallel", "arbitrary")),
        name="glu",
    )(w_glu_t, y_t, b_glu.reshape(w, 1).astype(F32))


def _outproj_kernel(oa_ref, ob_ref, oc_ref, ga_ref, gb_ref, gc_ref,
                    wa_ref, wb_ref, wc_ref, o_ref, na_ref, nb_ref, nc_ref):
    @pl.when(pl.program_id(1) == 0)
    def _():
        na_ref[...] = _rms_rows(oa_ref[...].astype(F32), ga_ref[...]).astype(BF16)
        nb_ref[...] = _rms_rows(ob_ref[...].astype(F32), gb_ref[...]).astype(BF16)
        oc = oc_ref[...].astype(F32)
        inv = lax.rsqrt(jnp.mean(oc * oc, axis=0, keepdims=True) + RMS_EPS)
        nc_ref[...] = (oc * inv * gc_ref[...]).astype(BF16)

    acc = jnp.dot(na_ref[...], wa_ref[...], preferred_element_type=F32)
    acc += jnp.dot(nb_ref[...], wb_ref[...], preferred_element_type=F32)
    acc += lax.dot_general(nc_ref[...], wc_ref[...], (((0,), (0,)), ((), ())),
                           preferred_element_type=F32)
    o_ref[...] = acc.astype(o_ref.dtype)


def _outproj(o_a, o_b, o_c_t, g_group, w_out, tm=512, tn=1024):
    n, wa = o_a.shape
    wb = o_b.shape[1]
    wc = o_c_t.shape[0]
    d = w_out.shape[1]
    tm, tn = _tile(n, tm), _tile(d, tn)
    assert wa % wb == 0 and (wa + wb) % wc == 0
    g = g_group.astype(F32)
    return pl.pallas_call(
        _outproj_kernel,
        out_shape=jax.ShapeDtypeStruct((n, d), BF16),
        grid=(n // tm, d // tn),
        in_specs=[pl.BlockSpec((tm, wa), lambda i, j: (i, 0)),
                  pl.BlockSpec((tm, wb), lambda i, j: (i, 0)),
                  pl.BlockSpec((wc, tm), lambda i, j: (0, i)),
                  pl.BlockSpec((1, wa), lambda i, j: (0, 0)),
                  pl.BlockSpec((1, wb), lambda i, j: (0, 0)),
                  pl.BlockSpec((wc, 1), lambda i, j: (0, 0)),
                  pl.BlockSpec((wa, tn), lambda i, j: (0, j)),
                  pl.BlockSpec((wb, tn), lambda i, j: (wa // wb, j)),
                  pl.BlockSpec((wc, tn), lambda i, j: ((wa + wb) // wc, j))],
        out_specs=pl.BlockSpec((tm, tn), lambda i, j: (i, j)),
        scratch_shapes=[pltpu.VMEM((tm, wa), BF16), pltpu.VMEM((tm, wb), BF16),
                        pltpu.VMEM((wc, tm), BF16)],
        compiler_params=_cparams(("parallel", "arbitrary")),
        name="outproj",
    )(o_a, o_b, o_c_t, g[:wa].reshape(1, wa), g[wa:wa + wb].reshape(1, wb),
      g[wa + wb:].reshape(wc, 1), w_out, w_out, w_out)


def _residual_kernel(y_ref, x_ref, gp_ref, gn_ref, xo_ref, ho_ref):
    xn = x_ref[...] + _rms_rows(y_ref[...].astype(F32), gp_ref[...])
    xo_ref[...] = xn
    ho_ref[...] = _rms_rows(xn, gn_ref[...]).astype(ho_ref.dtype)


def _residual_last_kernel(y_ref, x_ref, gp_ref, xo_ref):
    xo_ref[...] = x_ref[...] + _rms_rows(y_ref[...].astype(F32), gp_ref[...])


def _residual_norm(y, x, g_post, g_next, tm=256):
    n, d = x.shape
    tm = _tile(n, tm)
    row = pl.BlockSpec((tm, d), lambda i: (i, 0))
    vec = pl.BlockSpec((1, d), lambda i: (0, 0))
    gp = g_post.reshape(1, d).astype(F32)
    if g_next is None:
        return pl.pallas_call(
            _residual_last_kernel,
            out_shape=jax.ShapeDtypeStruct((n, d), F32),
            grid=(n // tm,), in_specs=[row, row, vec], out_specs=row,
            compiler_params=_cparams(("parallel",)), name="residual_last",
        )(y, x, gp), None
    return pl.pallas_call(
        _residual_kernel,
        out_shape=(jax.ShapeDtypeStruct((n, d), F32), jax.ShapeDtypeStruct((n, d), BF16)),
        grid=(n // tm,), in_specs=[row, row, vec, vec], out_specs=(row, row),
        compiler_params=_cparams(("parallel",)), name="residual_norm",
    )(y, x, gp, g_next.reshape(1, d).astype(F32))


def _ffn_kernel(h_ref, wg_ref, wu_ref, wd_ref, o_ref, acc_ref):
    f = pl.program_id(1)

    @pl.when(f == 0)
    def _():
        acc_ref[...] = jnp.zeros_like(acc_ref)

    h = h_ref[...]
    gate = jnp.dot(h, wg_ref[...], preferred_element_type=F32)
    up = jnp.dot(h, wu_ref[...], preferred_element_type=F32)
    act = (gate * jax.nn.sigmoid(gate) * up).astype(BF16)
    acc_ref[...] += jnp.dot(act, wd_ref[...], preferred_element_type=F32)

    @pl.when(f == pl.num_programs(1) - 1)
    def _():
        o_ref[...] = acc_ref[...].astype(o_ref.dtype)


def _ffn(h, w_gate, w_up, w_down, tm=1024, tf=256):
    n, d = h.shape
    d_ff = w_gate.shape[1]
    tm = _tile(n, tm)
    assert d_ff % tf == 0
    once = pl.Buffered(1)
    return pl.pallas_call(
        _ffn_kernel,
        out_shape=jax.ShapeDtypeStruct((n, d), BF16),
        grid=(n // tm, d_ff // tf),
        in_specs=[pl.BlockSpec((tm, d), lambda i, f: (i, 0), pipeline_mode=once),
                  pl.BlockSpec((d, tf), lambda i, f: (0, f)),
                  pl.BlockSpec((d, tf), lambda i, f: (0, f)),
                  pl.BlockSpec((tf, d), lambda i, f: (f, 0))],
        out_specs=pl.BlockSpec((tm, d), lambda i, f: (i, 0), pipeline_mode=once),
        scratch_shapes=[pltpu.VMEM((tm, d), F32)],
        compiler_params=_cparams(("parallel", "arbitrary")),
        name="ffn",
    )(h, w_gate, w_up, w_down)


def kernel(x, w_in, w_out, g_pre_mix, g_post_mix, g_group, rel_bias, b_forget,
           ssm_a_re, ssm_a_im, ssm_log_dt, ssm_b_re, ssm_b_im, ssm_c_re, ssm_c_im,
           ssm_d, w_glu, b_glu, g_pre_ffn, g_post_ffn, w_ffn_gate, w_ffn_up, w_ffn_down):
    batch, seq, d_model = x.shape
    depth = w_in.shape[0]
    a_heads = rel_bias.shape[1]
    b_heads = b_forget.shape[1]
    a_w, b_w = a_heads * HEAD_DIM, b_heads * HEAD_DIM
    ssm_w = ssm_d.shape[1]
    n_tok = batch * seq
    tok_cols = 3 * a_w + 3 * b_w
    assert w_in.shape[2] == tok_cols + b_heads + ssm_w
    assert seq % SSM_T == 0 and b_heads <= LANES
    scale = HEAD_DIM ** -0.5

    xf = x.reshape(n_tok, d_model).astype(F32)
    h = _rmsnorm(xf, g_pre_mix[0].astype(F32))
    for l in range(depth):
        wl = w_in[l].astype(F32)
        col_scale = jnp.concatenate([
            jnp.full((a_w,), scale, F32), jnp.ones((2 * a_w,), F32),
            jnp.full((b_w,), scale * LOG2E, F32), jnp.ones((2 * b_w,), F32)])
        w_f = jnp.pad(wl[:, tok_cols:tok_cols + b_heads],
                      ((0, 0), (0, LANES - b_heads))).astype(BF16)
        b_f = jnp.pad(b_forget[l].astype(F32), (0, LANES - b_heads)).reshape(1, LANES)
        w_uc_t = wl[:, tok_cols + b_heads:].T.astype(BF16)

        proj = _inproj_tok(h, wl, col_scale, tok_cols)
        u_t = _matmul_nt(w_uc_t, h, name="inproj_ssm")
        k_aug = _forget_bias(h, w_f, b_f, batch, seq, b_heads)

        o_a = _attention_a(proj, _attn_a_bias_vector(rel_bias[l]), batch, seq, a_heads,
                           0, a_heads, 2 * a_heads)
        c0 = 3 * a_heads
        o_b = _attention_b(proj, k_aug, batch, seq, b_heads,
                           c0, c0 + b_heads, c0 + 2 * b_heads)
        kmat, wst, cpw, a1, a2 = _ssm_prep(ssm_a_re[l], ssm_a_im[l], ssm_log_dt[l],
                                           ssm_b_re[l], ssm_b_im[l], ssm_c_re[l], ssm_c_im[l])
        y_t = _ssm(u_t, ssm_d[l], kmat, wst, cpw, a1, a2, seq)
        o_c_t = _glu(y_t, w_glu[l].astype(F32).T.astype(BF16), b_glu[l])

        mix = _outproj(o_a, o_b, o_c_t, g_group[l], w_out[l].astype(BF16))
        xf, h2 = _residual_norm(mix, xf, g_post_mix[l], g_pre_ffn[l])

        f = _ffn(h2, w_ffn_gate[l].astype(BF16), w_ffn_up[l].astype(BF16),
                 w_ffn_down[l].astype(BF16))
        g_next = g_pre_mix[l + 1] if l + 1 < depth else None
        xf, h = _residual_norm(f, xf, g_post_ffn[l], g_next)
    return xf.reshape(batch, seq, d_model).astype(x.dtype)
```

```python
import functools
import math

import jax
import jax.numpy as jnp
from jax import lax
from jax.experimental import pallas as pl
from jax.experimental.pallas import tpu as pltpu

CHUNK = 64
LEFT_CHUNKS = 8
REL_CLIP = 256
HEAD_DIM = 128
SSM_GROUP = 16
SSM_STATE = 64
RMS_EPS = 1e-6

LANES = 128
VMEM_LIMIT_BYTES = 56 * 1024 * 1024

SSM_T = 128
A_PAIR = 2 * CHUNK
A_WIN = A_PAIR + LEFT_CHUNKS * CHUNK
NEG_BIG = -1e30
LOG2E = math.log2(math.e)
B_AUG = 3

F32 = jnp.float32
BF16 = jnp.bfloat16


def _cparams(sem):
    return pltpu.CompilerParams(dimension_semantics=sem,
                                vmem_limit_bytes=VMEM_LIMIT_BYTES)


def _tile(n, want):
    t = min(n, want)
    while n % t:
        t //= 2
    return t


def _rms_rows(xf, g):
    return xf * lax.rsqrt(jnp.mean(xf * xf, axis=-1, keepdims=True) + RMS_EPS) * g


def _rmsnorm_kernel(x_ref, g_ref, o_ref):
    o_ref[...] = _rms_rows(x_ref[...], g_ref[...]).astype(o_ref.dtype)


def _rmsnorm(x, g, tm=256):
    n, d = x.shape
    tm = _tile(n, tm)
    return pl.pallas_call(
        _rmsnorm_kernel,
        out_shape=jax.ShapeDtypeStruct((n, d), BF16),
        grid=(n // tm,),
        in_specs=[pl.BlockSpec((tm, d), lambda i: (i, 0)),
                  pl.BlockSpec((1, d), lambda i: (0, 0))],
        out_specs=pl.BlockSpec((tm, d), lambda i: (i, 0)),
        compiler_params=_cparams(("parallel",)),
        name="rmsnorm",
    )(x, g.reshape(1, d))


def _inproj_tok_kernel(a_ref, w_ref, s_ref, o_ref, wb_ref):
    @pl.when(pl.program_id(1) == 0)
    def _():
        wb_ref[...] = (w_ref[...] * s_ref[...]).astype(BF16)

    o_ref[...] = jnp.dot(a_ref[...], wb_ref[...],
                         preferred_element_type=F32).astype(o_ref.dtype)


def _inproj_tok(a, w, layer, col_scale, n_cols, tm=1024, tn=512):
    m, k = a.shape
    tm, tn = _tile(m, tm), _tile(n_cols, tn)
    return pl.pallas_call(
        _inproj_tok_kernel,
        out_shape=jax.ShapeDtypeStruct((m, n_cols), BF16),
        grid=(n_cols // tn, m // tm),
        in_specs=[pl.BlockSpec((tm, k), lambda j, i: (i, 0)),
                  pl.BlockSpec((None, k, tn), lambda j, i: (layer, 0, j)),
                  pl.BlockSpec((1, tn), lambda j, i: (0, j))],
        out_specs=pl.BlockSpec((tm, tn), lambda j, i: (i, j)),
        scratch_shapes=[pltpu.VMEM((k, tn), BF16)],
        compiler_params=_cparams(("parallel", "arbitrary")),
        name="inproj_tok",
    )(a, w, col_scale.reshape(1, n_cols))


def _mm_nt_kernel(a_ref, b_ref, o_ref):
    o_ref[...] = lax.dot_general(a_ref[...], b_ref[...],
                                 (((1,), (1,)), ((), ())),
                                 preferred_element_type=F32).astype(o_ref.dtype)


def _matmul_nt(a, layer, b, tr=1024, tn=1024, name="matmul_nt"):
    _, r, k = a.shape
    n, _ = b.shape
    tr, tn = _tile(r, tr), _tile(n, tn)
    return pl.pallas_call(
        _mm_nt_kernel,
        out_shape=jax.ShapeDtypeStruct((r, n), BF16),
        grid=(n // tn, r // tr),
        in_specs=[pl.BlockSpec((None, tr, k), lambda i, j: (layer, j, 0)),
                  pl.BlockSpec((tn, k), lambda i, j: (i, 0))],
        out_specs=pl.BlockSpec((tr, tn), lambda i, j: (j, i)),
        compiler_params=_cparams(("parallel", "parallel")),
        name=name,
    )(a, b)


def _fgate_kernel(h_ref, w_ref, b_ref, o_ref, carry_ref, *, tiles_per_seq, heads):
    i = pl.program_id(0)

    @pl.when(i % tiles_per_seq == 0)
    def _():
        carry_ref[...] = jnp.zeros_like(carry_ref)

    logit = jnp.dot(h_ref[...], w_ref[...], preferred_element_type=F32) + b_ref[...]
    ls = -(jnp.maximum(-logit, 0.0) + jnp.log1p(jnp.exp(-jnp.abs(logit))))
    tm = ls.shape[0]
    row = lax.broadcasted_iota(jnp.int32, (tm, tm), 0)
    col = lax.broadcasted_iota(jnp.int32, (tm, tm), 1)
    tri = (col <= row).astype(F32)
    cs = jnp.dot(tri, ls, preferred_element_type=F32,
                 precision=lax.Precision.HIGHEST) + carry_ref[...]
    carry_ref[...] = cs[tm - 1:tm, :]
    lane = lax.broadcasted_iota(jnp.int32, (tm, LANES), 1)
    for hd in range(heads):
        x = cs[:, hd:hd + 1] * (-LOG2E)
        hi = x.astype(BF16).astype(F32)
        mid = (x - hi).astype(BF16).astype(F32)
        lo = x - hi - mid
        o_ref[hd] = jnp.where(lane == 0, hi, jnp.where(lane == 1, mid, jnp.where(
            lane == 2, lo, 0.0))).astype(o_ref.dtype)


def _forget_bias(h, w_f, b_f, batch, seq, heads, tm=512):
    n, d = h.shape
    tm = _tile(seq, tm)
    tps = seq // tm
    return pl.pallas_call(
        functools.partial(_fgate_kernel, tiles_per_seq=tps, heads=heads),
        out_shape=jax.ShapeDtypeStruct((batch * heads, seq, LANES), BF16),
        grid=(n // tm,),
        in_specs=[pl.BlockSpec((tm, d), lambda i: (i, 0)),
                  pl.BlockSpec((d, LANES), lambda i: (0, 0)),
                  pl.BlockSpec((1, LANES), lambda i: (0, 0))],
        out_specs=pl.BlockSpec((heads, tm, LANES), lambda i: (i // tps, i % tps, 0)),
        scratch_shapes=[pltpu.VMEM((1, LANES), F32)],
        compiler_params=_cparams(("arbitrary",)),
        name="forget_bias",
    )(h, w_f, b_f)


def _attn_a_kernel(q_ref, kp_ref, kc_ref, vp_ref, vc_ref, ext_ref, o_ref, bias_ref, *, tq):
    qi = pl.program_id(2)
    n_pairs = tq // A_PAIR

    @pl.when(qi == 0)
    def _():
        r = lax.broadcasted_iota(jnp.int32, (A_PAIR, A_PAIR), 0)
        c = lax.broadcasted_iota(jnp.int32, (A_PAIR, A_PAIR), 1)
        n_seg = A_WIN // A_PAIR + 1
        rolled = [pltpu.roll(jnp.broadcast_to(
            ext_ref[0, :, j * A_PAIR:(j + 1) * A_PAIR], (A_PAIR, A_PAIR)), 0, 1,
            stride=1, stride_axis=0) for j in range(n_seg)]
        lo = (r // CHUNK) * CHUNK
        for j in range(n_seg - 1):
            tile = jnp.where(c >= r, rolled[j + 1], rolled[j])
            cc = c + j * A_PAIR
            band = jnp.logical_and(cc >= lo, cc < lo + (LEFT_CHUNKS + 1) * CHUNK)
            bias_ref[:, j * A_PAIR:(j + 1) * A_PAIR] = jnp.where(band, tile, NEG_BIG)

    bias = bias_ref[...]
    col = lax.broadcasted_iota(jnp.int32, (A_PAIR, A_WIN), 1)
    for p in range(n_pairs):
        lo = p * A_PAIR
        ws = tq + lo - LEFT_CHUNKS * CHUNK
        q = q_ref[lo:lo + A_PAIR, :]
        k_parts, v_parts = [], []
        if ws < tq:
            k_parts.append(kp_ref[ws:tq, :])
            v_parts.append(vp_ref[ws:tq, :])
        k_parts.append(kc_ref[max(ws - tq, 0):lo + A_PAIR, :])
        v_parts.append(vc_ref[max(ws - tq, 0):lo + A_PAIR, :])
        k = jnp.concatenate(k_parts, axis=0) if len(k_parts) > 1 else k_parts[0]
        v = jnp.concatenate(v_parts, axis=0) if len(v_parts) > 1 else v_parts[0]
        s = lax.dot_general(q, k, (((1,), (1,)), ((), ())),
                            preferred_element_type=F32) + bias
        s = jnp.where(jnp.logical_or(qi > 0, col >= tq - ws), s, NEG_BIG)
        m = jnp.max(s, axis=-1, keepdims=True)
        e = jnp.exp(s - m)
        l = jnp.sum(e, axis=-1, keepdims=True)
        o = jnp.dot(e.astype(BF16), v, preferred_element_type=F32)
        o_ref[lo:lo + A_PAIR, :] = (o / l).astype(o_ref.dtype)


def _attention_a(proj, bias_ext, batch, seq, heads, q_col, k_col, v_col, tq=512):
    tq = _tile(seq, tq)
    assert tq >= LEFT_CHUNKS * CHUNK and tq % A_PAIR == 0
    nq = seq // tq

    def cur(col0):
        return pl.BlockSpec((tq, HEAD_DIM), lambda b, h, i: (b * nq + i, col0 + h))

    def prev(col0):
        return pl.BlockSpec((tq, HEAD_DIM),
                            lambda b, h, i: (b * nq + jnp.maximum(i - 1, 0), col0 + h))

    return pl.pallas_call(
        functools.partial(_attn_a_kernel, tq=tq),
        out_shape=jax.ShapeDtypeStruct((batch * seq, heads * HEAD_DIM), BF16),
        grid=(batch, heads, nq),
        in_specs=[cur(q_col), prev(k_col), cur(k_col), prev(v_col), cur(v_col),
                  pl.BlockSpec((1, 1, A_WIN + A_PAIR), lambda b, h, i: (h, 0, 0))],
        out_specs=pl.BlockSpec((tq, HEAD_DIM), lambda b, h, i: (b * nq + i, h)),
        scratch_shapes=[pltpu.VMEM((A_PAIR, A_WIN), F32)],
        compiler_params=_cparams(("parallel", "parallel", "arbitrary")),
        name="attention_a",
    )(proj, proj, proj, proj, proj, bias_ext)


def _attn_a_bias_vector(rel_bias):
    x = jnp.arange(A_WIN + A_PAIR)
    idx = jnp.clip(LEFT_CHUNKS * CHUNK + A_PAIR - x, -REL_CLIP, REL_CLIP) + REL_CLIP
    return rel_bias.astype(F32)[:, idx].reshape(rel_bias.shape[0], 1, A_WIN + A_PAIR)


def _attn_b_kernel(q_ref, k_ref, v_ref, aug_ref, o_ref, acc_ref, s_ref, m_ref, *, tq, sub):
    qi = pl.program_id(2)
    n_sub = tq // sub
    nt = (((1,), (1,)), ((), ()))
    lane = lax.broadcasted_iota(jnp.int32, (tq, HEAD_DIM), 1)
    qx = jnp.concatenate([q_ref[...], jnp.where(lane < B_AUG, 1.0, 0.0).astype(BF16)],
                         axis=1)
    ones = jnp.ones((tq, HEAD_DIM), BF16)

    def key_block(ki):
        rows = pl.ds(pl.multiple_of(ki * tq, tq), tq)
        return jnp.concatenate([k_ref[rows, :], aug_ref[0, rows, :]], axis=1)

    def value_block(ki):
        rows = pl.ds(pl.multiple_of(ki * tq, tq), tq)
        return jnp.concatenate([v_ref[rows, :], ones], axis=1)

    def absorb(r, vx, n_keys, masked):
        rows = slice(r * sub, (r + 1) * sub)
        s = s_ref[rows, :n_keys]
        if masked:
            row = lax.broadcasted_iota(jnp.int32, (sub, n_keys), 0) + r * sub
            col = lax.broadcasted_iota(jnp.int32, (sub, n_keys), 1)
            s = jnp.where(col <= row, s, NEG_BIG)
        m_old = m_ref[rows, :]
        m_new = jnp.maximum(m_old, jnp.max(s, axis=-1, keepdims=True))
        e = jnp.exp2(s - m_new).astype(BF16)
        acc_ref[rows, :] = jnp.exp2(m_old - m_new) * acc_ref[rows, :] + jnp.dot(
            e, vx[:n_keys], preferred_element_type=F32)
        m_ref[rows, :] = m_new

    acc_ref[...] = jnp.zeros_like(acc_ref)
    m_ref[...] = jnp.full_like(m_ref, -jnp.inf)
    s_ref[...] = lax.dot_general(qx, key_block(0), nt, preferred_element_type=F32)

    def body(ki, carry):
        kx = key_block(ki + 1)
        vx = value_block(ki)
        for r in range(n_sub):
            rows = slice(r * sub, (r + 1) * sub)
            s_next = lax.dot_general(qx[rows], kx, nt, preferred_element_type=F32)
            absorb(r, vx, tq, False)
            s_ref[rows, :] = s_next
        return carry

    lax.fori_loop(0, qi, body, 0)
    vx = value_block(qi)
    for r in range(n_sub):
        absorb(r, vx, (r + 1) * sub, True)
    acc = acc_ref[...]
    o_ref[...] = (acc[:, :HEAD_DIM] / acc[:, HEAD_DIM:HEAD_DIM + 1]).astype(o_ref.dtype)


def _attention_b(proj, k_aug, batch, seq, heads, q_col, k_col, v_col, tq=1024, sub=256):
    tq = _tile(seq, tq)
    sub = _tile(tq, sub)
    nq = seq // tq
    return pl.pallas_call(
        functools.partial(_attn_b_kernel, tq=tq, sub=sub),
        out_shape=jax.ShapeDtypeStruct((batch * seq, heads * HEAD_DIM), BF16),
        grid=(batch, heads, nq),
        in_specs=[pl.BlockSpec((tq, HEAD_DIM), lambda b, h, i: (b * nq + i, q_col + h)),
                  pl.BlockSpec((seq, HEAD_DIM), lambda b, h, i: (b, k_col + h)),
                  pl.BlockSpec((seq, HEAD_DIM), lambda b, h, i: (b, v_col + h)),
                  pl.BlockSpec((1, seq, LANES), lambda b, h, i: (b * heads + h, 0, 0))],
        out_specs=pl.BlockSpec((tq, HEAD_DIM), lambda b, h, i: (b * nq + i, h)),
        scratch_shapes=[pltpu.VMEM((tq, 2 * HEAD_DIM), F32), pltpu.VMEM((tq, tq), F32),
                        pltpu.VMEM((tq, 1), F32)],
        compiler_params=_cparams(("parallel", "parallel", "arbitrary")),
        name="attention_b",
    )(proj, proj, proj, k_aug)


def _ssm_operators(are_ref, aim_ref, ldt_ref, btr_ref, bti_ref, cr_ref, ci_ref,
                   kmat_ref, wst_ref, cpw_ref, ktab_ref):
    t_len, n_st, n_ch = SSM_T, SSM_STATE, SSM_GROUP
    lr = jnp.minimum(are_ref[0], -1e-4)
    li = aim_ref[0]
    dt = jnp.exp(ldt_ref[0])
    xr, xi = lr * dt, li * dt

    def power(tau):
        mag = jnp.exp(tau * xr)
        return mag * jnp.cos(tau * xi), mag * jnp.sin(tau * xi)

    one = jnp.ones((1, 1), F32)
    er, ei = power(one)
    den = lr * lr + li * li
    nr, ni = er - 1.0, ei
    cfr = (nr * lr + ni * li) / den
    cfi = (ni * lr - nr * li) / den
    btr, bti = btr_ref[0], bti_ref[0]
    bbr = cfr * btr - cfi * bti
    bbi = cfr * bti + cfi * btr
    cr, ci = cr_ref[0], ci_ref[0]

    tau = lax.broadcasted_iota(jnp.int32, (t_len, 2 * n_st), 0).astype(F32)
    lane = lax.broadcasted_iota(jnp.int32, (t_len, 2 * n_st), 1)
    tau = jnp.where(lane < n_st, tau, float(t_len - 1) - tau)
    mag = jnp.exp(tau * jnp.concatenate([xr, xr], axis=1))
    ang = tau * jnp.concatenate([xi, xi], axis=1)
    pw_r, pw_i = mag * jnp.cos(ang), mag * jnp.sin(ang)
    pr, pi = pw_r[:, :n_st], pw_i[:, :n_st]
    rr, ri = pw_r[:, n_st:], pw_i[:, n_st:]
    qr, qi = pr * er - pi * ei, pr * ei + pi * er

    w_rows = []
    for ip in range(n_ch):
        c_r, c_i = cr[ip:ip + 1, :], ci[ip:ip + 1, :]
        w_rows.append(jnp.concatenate([c_r * bbr - c_i * bbi,
                                       -(c_r * bbi + c_i * bbr)], axis=1))
    w_cat = jnp.concatenate(w_rows, axis=0)
    p_cat = jnp.concatenate([pr, pi], axis=1)
    ktab_ref[...] = lax.dot_general(w_cat, p_cat, (((1,), (1,)), ((), ())),
                                    preferred_element_type=F32,
                                    precision=lax.Precision.HIGHEST)

    row = lax.broadcasted_iota(jnp.int32, (t_len, t_len), 0)
    col = lax.broadcasted_iota(jnp.int32, (t_len, t_len), 1)
    causal = col >= row

    def fill(i, carry):
        r0 = pl.multiple_of(i * t_len, t_len)
        for ip in range(n_ch):
            kvec = ktab_ref[pl.ds(ip * n_ch + i, 1), :]
            blk = pltpu.roll(jnp.broadcast_to(kvec, (t_len, t_len)), 0, 1,
                             stride=1, stride_axis=0)
            kmat_ref[pl.ds(r0, t_len), ip * t_len:(ip + 1) * t_len] = (
                jnp.where(causal, blk, 0.0).astype(kmat_ref.dtype))
        return carry

    lax.fori_loop(0, n_ch, fill, 0)

    for i in range(n_ch):
        br, bi = bbr[i:i + 1, :], bbi[i:i + 1, :]
        wst_ref[i * t_len:(i + 1) * t_len, :] = jnp.concatenate(
            [rr * br - ri * bi, rr * bi + ri * br], axis=1).astype(wst_ref.dtype)
        c_r, c_i = cr[i:i + 1, :], ci[i:i + 1, :]
        cpw_ref[i * t_len:(i + 1) * t_len, :] = jnp.concatenate(
            [c_r * qr - c_i * qi, -(c_r * qi + c_i * qr)], axis=1).astype(cpw_ref.dtype)

    kk = lax.broadcasted_iota(jnp.int32, (8, 1), 0)
    ar, ai = power(float(t_len) * jnp.exp2(kk.astype(F32)))
    return jnp.concatenate([ar, ar], axis=1), jnp.concatenate([-ai, ai], axis=1)


def _gelu_tanh(x):
    return 0.5 * x * (1.0 + jnp.tanh(math.sqrt(2.0 / math.pi) * (x + 0.044715 * x * x * x)))


def _ssm_kernel(d_ref, are_ref, aim_ref, ldt_ref, btr_ref, bti_ref, cr_ref, ci_ref, u_ref,
                o_ref, kmat_ref, wst_ref, cpw_ref, ktab_ref, *, chunks_per_seq):
    g = pl.program_id(0)
    n_ch, t_len, n2 = SSM_GROUP, SSM_T, 2 * SSM_STATE
    a1, a2 = _ssm_operators(are_ref, aim_ref, ldt_ref, btr_ref, bti_ref, cr_ref, ci_ref,
                            kmat_ref, wst_ref, cpw_ref, ktab_ref)
    x = jnp.concatenate([u_ref[i] for i in range(n_ch)], axis=1)
    n_rows = x.shape[0]
    y = jnp.dot(x, kmat_ref[...], preferred_element_type=F32)
    e = jnp.dot(x, wst_ref[...], preferred_element_type=F32)

    pos = lax.broadcasted_iota(jnp.int32, (n_rows, n2), 0) % chunks_per_seq
    step, k = 1, 0
    while step < chunks_per_seq:
        sh = jnp.where(pos >= step, pltpu.roll(e, step, 0), 0.0)
        e = e + a1[k:k + 1, :] * sh + a2[k:k + 1, :] * pltpu.roll(sh, SSM_STATE, 1)
        step, k = step * 2, k + 1
    s_start = jnp.where(pos >= 1, pltpu.roll(e, 1, 0), 0.0)
    y = y + lax.dot_general(s_start.astype(BF16), cpw_ref[...],
                            (((1,), (1,)), ((), ())), preferred_element_type=F32)
    for i in range(n_ch):
        yi = y[:, i * t_len:(i + 1) * t_len] + d_ref[g * n_ch + i] * u_ref[i].astype(F32)
        o_ref[i] = _gelu_tanh(yi).astype(o_ref.dtype)


def _ssm(u_t, d_skip, a_re, a_im, log_dt, b_re, b_im, c_re, c_im, seq):
    w, n = u_t.shape
    g, n_st = a_re.shape
    n_ch, t_len = SSM_GROUP, SSM_T
    c = n // t_len
    assert g * n_ch == w and c // (n // seq) <= 2 ** 8
    gw = n_ch * t_len
    u3 = u_t.reshape(w, c, t_len)
    vec = lambda a: a.reshape(g, 1, -1).astype(F32)
    bt = lambda a: jnp.swapaxes(a, 1, 2).astype(F32)
    spec3 = lambda s1, s2: pl.BlockSpec((1, s1, s2), lambda i: (i, 0, 0))
    out = pl.pallas_call(
        functools.partial(_ssm_kernel, chunks_per_seq=seq // t_len),
        out_shape=jax.ShapeDtypeStruct((w, c, t_len), BF16),
        grid=(g,),
        in_specs=[pl.BlockSpec(memory_space=pltpu.SMEM),
                  spec3(1, n_st), spec3(1, n_st), spec3(1, 1),
                  spec3(n_ch, n_st), spec3(n_ch, n_st),
                  spec3(n_ch, n_st), spec3(n_ch, n_st),
                  pl.BlockSpec((n_ch, c, t_len), lambda i: (i, 0, 0))],
        out_specs=pl.BlockSpec((n_ch, c, t_len), lambda i: (i, 0, 0)),
        scratch_shapes=[pltpu.VMEM((gw, gw), BF16), pltpu.VMEM((gw, 2 * n_st), BF16),
                        pltpu.VMEM((gw, 2 * n_st), BF16), pltpu.VMEM((n_ch * n_ch, t_len), F32)],
        compiler_params=_cparams(("parallel",)),
        name="ssm",
    )(d_skip.astype(F32), vec(a_re), vec(a_im), log_dt.reshape(g, 1, 1).astype(F32),
      bt(b_re), bt(b_im), c_re.astype(F32), c_im.astype(F32), u3)
    return out.reshape(w, n)


def _glu_kernel(w_ref, y_ref, b_ref, o_ref, *, tr):
    j = pl.program_id(1)
    z = jnp.dot(w_ref[...], y_ref[...], preferred_element_type=F32) + b_ref[...]
    rows = pl.ds(pl.multiple_of(j * tr, tr), tr)
    o_ref[...] = (y_ref[rows, :].astype(F32) * jax.nn.sigmoid(z)).astype(o_ref.dtype)


def _glu(y_t, w_glu_t, layer, b_glu, tr=1024, tn=1024):
    w, n = y_t.shape
    tr, tn = _tile(w, tr), _tile(n, tn)
    return pl.pallas_call(
        functools.partial(_glu_kernel, tr=tr),
        out_shape=jax.ShapeDtypeStruct((w, n), BF16),
        grid=(n // tn, w // tr),
        in_specs=[pl.BlockSpec((None, tr, w), lambda i, j: (layer, j, 0)),
                  pl.BlockSpec((w, tn), lambda i, j: (0, i)),
                  pl.BlockSpec((tr, 1), lambda i, j: (j, 0))],
        out_specs=pl.BlockSpec((tr, tn), lambda i, j: (j, i)),
        compiler_params=_cparams(("parallel", "arbitrary")),
        name="glu",
    )(w_glu_t, y_t, b_glu.reshape(w, 1).astype(F32))


def _outproj_kernel(oa_ref, ob_ref, oc_ref, ga_ref, gb_ref, gc_ref,
                    wa_ref, wb_ref, wc_ref, o_ref, na_ref, nb_ref, nc_ref):
    @pl.when(pl.program_id(1) == 0)
    def _():
        na_ref[...] = _rms_rows(oa_ref[...].astype(F32), ga_ref[...]).astype(BF16)
        nb_ref[...] = _rms_rows(ob_ref[...].astype(F32), gb_ref[...]).astype(BF16)
        oc = oc_ref[...].astype(F32)
        inv = lax.rsqrt(jnp.mean(oc * oc, axis=0, keepdims=True) + RMS_EPS)
        nc_ref[...] = (oc * inv * gc_ref[...]).astype(BF16)

    acc = jnp.dot(na_ref[...], wa_ref[...], preferred_element_type=F32)
    acc += jnp.dot(nb_ref[...], wb_ref[...], preferred_element_type=F32)
    acc += lax.dot_general(nc_ref[...], wc_ref[...], (((0,), (0,)), ((), ())),
                           preferred_element_type=F32)
    o_ref[...] = acc.astype(o_ref.dtype)


def _outproj(o_a, o_b, o_c_t, g_group, w_out, layer, tm=512, tn=1024):
    n, wa = o_a.shape
    wb = o_b.shape[1]
    wc = o_c_t.shape[0]
    d = w_out.shape[2]
    tm, tn = _tile(n, tm), _tile(d, tn)
    assert wa % wb == 0 and (wa + wb) % wc == 0
    g = g_group.astype(F32)
    return pl.pallas_call(
        _outproj_kernel,
        out_shape=jax.ShapeDtypeStruct((n, d), BF16),
        grid=(n // tm, d // tn),
        in_specs=[pl.BlockSpec((tm, wa), lambda i, j: (i, 0)),
                  pl.BlockSpec((tm, wb), lambda i, j: (i, 0)),
                  pl.BlockSpec((wc, tm), lambda i, j: (0, i)),
                  pl.BlockSpec((1, wa), lambda i, j: (0, 0)),
                  pl.BlockSpec((1, wb), lambda i, j: (0, 0)),
                  pl.BlockSpec((wc, 1), lambda i, j: (0, 0)),
                  pl.BlockSpec((None, wa, tn), lambda i, j: (layer, 0, j)),
                  pl.BlockSpec((None, wb, tn), lambda i, j: (layer, wa // wb, j)),
                  pl.BlockSpec((None, wc, tn), lambda i, j: (layer, (wa + wb) // wc, j))],
        out_specs=pl.BlockSpec((tm, tn), lambda i, j: (i, j)),
        scratch_shapes=[pltpu.VMEM((tm, wa), BF16), pltpu.VMEM((tm, wb), BF16),
                        pltpu.VMEM((wc, tm), BF16)],
        compiler_params=_cparams(("parallel", "arbitrary")),
        name="outproj",
    )(o_a, o_b, o_c_t, g[:wa].reshape(1, wa), g[wa:wa + wb].reshape(1, wb),
      g[wa + wb:].reshape(wc, 1), w_out, w_out, w_out)


def _residual_kernel(y_ref, x_ref, gp_ref, gn_ref, xo_ref, ho_ref):
    xn = x_ref[...] + _rms_rows(y_ref[...].astype(F32), gp_ref[...])
    xo_ref[...] = xn
    ho_ref[...] = _rms_rows(xn, gn_ref[...]).astype(ho_ref.dtype)


def _residual_last_kernel(y_ref, x_ref, gp_ref, xo_ref):
    xo_ref[...] = x_ref[...] + _rms_rows(y_ref[...].astype(F32), gp_ref[...])


def _residual_norm(y, x, g_post, g_next, tm=256):
    n, d = x.shape
    tm = _tile(n, tm)
    row = pl.BlockSpec((tm, d), lambda i: (i, 0))
    vec = pl.BlockSpec((1, d), lambda i: (0, 0))
    gp = g_post.reshape(1, d).astype(F32)
    if g_next is None:
        return pl.pallas_call(
            _residual_last_kernel,
            out_shape=jax.ShapeDtypeStruct((n, d), F32),
            grid=(n // tm,), in_specs=[row, row, vec], out_specs=row,
            compiler_params=_cparams(("parallel",)), name="residual_last",
        )(y, x, gp), None
    return pl.pallas_call(
        _residual_kernel,
        out_shape=(jax.ShapeDtypeStruct((n, d), F32), jax.ShapeDtypeStruct((n, d), BF16)),
        grid=(n // tm,), in_specs=[row, row, vec, vec], out_specs=(row, row),
        compiler_params=_cparams(("parallel",)), name="residual_norm",
    )(y, x, gp, g_next.reshape(1, d).astype(F32))


def _ffn_kernel(h_ref, wg_ref, wu_ref, wd_ref, o_ref, acc_ref):
    f = pl.program_id(1)

    @pl.when(f == 0)
    def _():
        acc_ref[...] = jnp.zeros_like(acc_ref)

    h = h_ref[...]
    gate = jnp.dot(h, wg_ref[...], preferred_element_type=F32)
    up = jnp.dot(h, wu_ref[...], preferred_element_type=F32)
    act = (gate * jax.nn.sigmoid(gate) * up).astype(BF16)
    acc_ref[...] += jnp.dot(act, wd_ref[...], preferred_element_type=F32)

    @pl.when(f == pl.num_programs(1) - 1)
    def _():
        o_ref[...] = acc_ref[...].astype(o_ref.dtype)


def _ffn(h, w_gate, w_up, w_down, layer, tm=1024, tf=256):
    n, d = h.shape
    d_ff = w_gate.shape[2]
    tm = _tile(n, tm)
    assert d_ff % tf == 0
    once = pl.Buffered(1)
    return pl.pallas_call(
        _ffn_kernel,
        out_shape=jax.ShapeDtypeStruct((n, d), BF16),
        grid=(n // tm, d_ff // tf),
        in_specs=[pl.BlockSpec((tm, d), lambda i, f: (i, 0), pipeline_mode=once),
                  pl.BlockSpec((None, d, tf), lambda i, f: (layer, 0, f)),
                  pl.BlockSpec((None, d, tf), lambda i, f: (layer, 0, f)),
                  pl.BlockSpec((None, tf, d), lambda i, f: (layer, f, 0))],
        out_specs=pl.BlockSpec((tm, d), lambda i, f: (i, 0), pipeline_mode=once),
        scratch_shapes=[pltpu.VMEM((tm, d), F32)],
        compiler_params=_cparams(("parallel", "arbitrary")),
        name="ffn",
    )(h, w_gate, w_up, w_down)


def kernel(x, w_in, w_out, g_pre_mix, g_post_mix, g_group, rel_bias, b_forget,
           ssm_a_re, ssm_a_im, ssm_log_dt, ssm_b_re, ssm_b_im, ssm_c_re, ssm_c_im,
           ssm_d, w_glu, b_glu, g_pre_ffn, g_post_ffn, w_ffn_gate, w_ffn_up, w_ffn_down):
    batch, seq, d_model = x.shape
    depth = w_in.shape[0]
    a_heads = rel_bias.shape[1]
    b_heads = b_forget.shape[1]
    a_w, b_w = a_heads * HEAD_DIM, b_heads * HEAD_DIM
    ssm_w = ssm_d.shape[1]
    n_tok = batch * seq
    tok_cols = 3 * a_w + 3 * b_w
    assert w_in.shape[2] == tok_cols + b_heads + ssm_w
    assert seq % SSM_T == 0 and b_heads <= LANES
    scale = HEAD_DIM ** -0.5

    w_in = w_in.astype(F32)
    col_scale = jnp.concatenate([
        jnp.full((a_w,), scale, F32), jnp.ones((2 * a_w,), F32),
        jnp.full((b_w,), scale * LOG2E, F32), jnp.ones((2 * b_w,), F32)])
    w_f = jnp.pad(w_in[:, :, tok_cols:tok_cols + b_heads],
                  ((0, 0), (0, 0), (0, LANES - b_heads))).astype(BF16)
    b_f = jnp.pad(b_forget.astype(F32), ((0, 0), (0, LANES - b_heads)))
    w_uc_t = jnp.swapaxes(w_in[:, :, tok_cols + b_heads:], 1, 2).astype(BF16)
    w_glu_t = jnp.swapaxes(w_glu, 1, 2).astype(BF16)
    w_out_b = w_out.astype(BF16)
    w_gate_b, w_up_b, w_down_b = (w.astype(BF16) for w in (w_ffn_gate, w_ffn_up, w_ffn_down))

    xf = x.reshape(n_tok, d_model).astype(F32)
    h = _rmsnorm(xf, g_pre_mix[0].astype(F32))
    for l in range(depth):
        proj = _inproj_tok(h, w_in, l, col_scale, tok_cols)
        u_t = _matmul_nt(w_uc_t, l, h, name="inproj_ssm")
        k_aug = _forget_bias(h, w_f[l], b_f[l:l + 1], batch, seq, b_heads)

        o_a = _attention_a(proj, _attn_a_bias_vector(rel_bias[l]), batch, seq, a_heads,
                           0, a_heads, 2 * a_heads)
        c0 = 3 * a_heads
        o_b = _attention_b(proj, k_aug, batch, seq, b_heads,
                           c0, c0 + b_heads, c0 + 2 * b_heads)
        y_t = _ssm(u_t, ssm_d[l], ssm_a_re[l], ssm_a_im[l], ssm_log_dt[l],
                   ssm_b_re[l], ssm_b_im[l], ssm_c_re[l], ssm_c_im[l], seq)
        o_c_t = _glu(y_t, w_glu_t, l, b_glu[l])

        mix = _outproj(o_a, o_b, o_c_t, g_group[l], w_out_b, l)
        xf, h2 = _residual_norm(mix, xf, g_post_mix[l], g_pre_ffn[l])

        f = _ffn(h2, w_gate_b, w_up_b, w_down_b, l)
        g_next = g_pre_mix[l + 1] if l + 1 < depth else None
        xf, h = _residual_norm(f, xf, g_post_ffn[l], g_next)
    return xf.reshape(batch, seq, d_model).astype(x.dtype)
```

```python
import functools
import math

import jax
import jax.numpy as jnp
from jax import lax
from jax.experimental import pallas as pl
from jax.experimental.pallas import tpu as pltpu

CHUNK = 64
LEFT_CHUNKS = 8
REL_CLIP = 256
HEAD_DIM = 128
SSM_GROUP = 16
SSM_STATE = 64
RMS_EPS = 1e-6

LANES = 128
VMEM_LIMIT_BYTES = 56 * 1024 * 1024

SSM_T = 64
A_PAIR = 2 * CHUNK
A_WIN = A_PAIR + LEFT_CHUNKS * CHUNK
NEG_BIG = -1e30
LOG2E = math.log2(math.e)
B_AUG = 3

F32 = jnp.float32
BF16 = jnp.bfloat16


def _cparams(sem):
    return pltpu.CompilerParams(dimension_semantics=sem,
                                vmem_limit_bytes=VMEM_LIMIT_BYTES)


def _tile(n, want):
    t = min(n, want)
    while n % t:
        t //= 2
    return t


def _rms_rows(xf, g):
    return xf * lax.rsqrt(jnp.mean(xf * xf, axis=-1, keepdims=True) + RMS_EPS) * g


def _rmsnorm_kernel(x_ref, g_ref, o_ref):
    o_ref[...] = _rms_rows(x_ref[...], g_ref[...]).astype(o_ref.dtype)


def _rmsnorm(x, g, tm=256):
    n, d = x.shape
    tm = _tile(n, tm)
    return pl.pallas_call(
        _rmsnorm_kernel,
        out_shape=jax.ShapeDtypeStruct((n, d), BF16),
        grid=(n // tm,),
        in_specs=[pl.BlockSpec((tm, d), lambda i: (i, 0)),
                  pl.BlockSpec((1, d), lambda i: (0, 0))],
        out_specs=pl.BlockSpec((tm, d), lambda i: (i, 0)),
        compiler_params=_cparams(("parallel",)),
        name="rmsnorm",
    )(x, g.reshape(1, d))


def _inproj_tok_kernel(a_ref, w_ref, s_ref, o_ref, wb_ref):
    @pl.when(pl.program_id(1) == 0)
    def _():
        wb_ref[...] = (w_ref[...] * s_ref[...]).astype(BF16)

    o_ref[...] = jnp.dot(a_ref[...], wb_ref[...],
                         preferred_element_type=F32).astype(o_ref.dtype)


def _inproj_tok(a, w, layer, col_scale, n_cols, tm=1024, tn=512):
    m, k = a.shape
    tm, tn = _tile(m, tm), _tile(n_cols, tn)
    return pl.pallas_call(
        _inproj_tok_kernel,
        out_shape=jax.ShapeDtypeStruct((m, n_cols), BF16),
        grid=(n_cols // tn, m // tm),
        in_specs=[pl.BlockSpec((tm, k), lambda j, i: (i, 0)),
                  pl.BlockSpec((None, k, tn), lambda j, i: (layer, 0, j)),
                  pl.BlockSpec((1, tn), lambda j, i: (0, j))],
        out_specs=pl.BlockSpec((tm, tn), lambda j, i: (i, j)),
        scratch_shapes=[pltpu.VMEM((k, tn), BF16)],
        compiler_params=_cparams(("parallel", "arbitrary")),
        name="inproj_tok",
    )(a, w, col_scale.reshape(1, n_cols))


def _uc_weight_kernel(a_ref, b_ref, o_ref, *, shift):
    at = a_ref[...].T
    bt = b_ref[...].T
    o_ref[...] = jnp.concatenate([at[shift:], bt[:shift]], axis=0).astype(o_ref.dtype)


def _uc_weight_t(w_in, col0, shift, n_rows, tr=1024, tk=512):
    depth, k, _ = w_in.shape
    tr, tk = _tile(n_rows, tr), _tile(k, tk)
    assert col0 % tr == 0 and tr % LANES == 0 and 0 < shift < LANES
    cb, nb = col0 // tr, tr // LANES
    return pl.pallas_call(
        functools.partial(_uc_weight_kernel, shift=shift),
        out_shape=jax.ShapeDtypeStruct((depth, n_rows, k), BF16),
        grid=(depth, n_rows // tr, k // tk),
        in_specs=[pl.BlockSpec((None, tk, tr), lambda l, j, kb: (l, kb, cb + j)),
                  pl.BlockSpec((None, tk, LANES), lambda l, j, kb: (l, kb, (cb + j + 1) * nb))],
        out_specs=pl.BlockSpec((None, tr, tk), lambda l, j, kb: (l, j, kb)),
        compiler_params=_cparams(("parallel", "parallel", "parallel")),
        name="uc_weight_transpose",
    )(w_in, w_in)


def _mm_nt_kernel(a_ref, b_ref, o_ref):
    o_ref[...] = lax.dot_general(a_ref[...], b_ref[...],
                                 (((1,), (1,)), ((), ())),
                                 preferred_element_type=F32).astype(o_ref.dtype)


def _matmul_nt(a, layer, b, tr=1024, tn=1024, name="matmul_nt"):
    _, r, k = a.shape
    n, _ = b.shape
    tr, tn = _tile(r, tr), _tile(n, tn)
    return pl.pallas_call(
        _mm_nt_kernel,
        out_shape=jax.ShapeDtypeStruct((r, n), BF16),
        grid=(n // tn, r // tr),
        in_specs=[pl.BlockSpec((None, tr, k), lambda i, j: (layer, j, 0)),
                  pl.BlockSpec((tn, k), lambda i, j: (i, 0))],
        out_specs=pl.BlockSpec((tr, tn), lambda i, j: (j, i)),
        compiler_params=_cparams(("parallel", "parallel")),
        name=name,
    )(a, b)


def _fgate_kernel(h_ref, w_ref, b_ref, o_ref, carry_ref, *, tiles_per_seq, heads):
    i = pl.program_id(0)

    @pl.when(i % tiles_per_seq == 0)
    def _():
        carry_ref[...] = jnp.zeros_like(carry_ref)

    logit = jnp.dot(h_ref[...], w_ref[...], preferred_element_type=F32) + b_ref[...]
    ls = -(jnp.maximum(-logit, 0.0) + jnp.log1p(jnp.exp(-jnp.abs(logit))))
    tm = ls.shape[0]
    row = lax.broadcasted_iota(jnp.int32, (tm, tm), 0)
    col = lax.broadcasted_iota(jnp.int32, (tm, tm), 1)
    tri = (col <= row).astype(F32)
    cs = jnp.dot(tri, ls, preferred_element_type=F32,
                 precision=lax.Precision.HIGHEST) + carry_ref[...]
    carry_ref[...] = cs[tm - 1:tm, :]
    lane = lax.broadcasted_iota(jnp.int32, (tm, LANES), 1)
    for hd in range(heads):
        x = cs[:, hd:hd + 1] * (-LOG2E)
        hi = x.astype(BF16).astype(F32)
        mid = (x - hi).astype(BF16).astype(F32)
        lo = x - hi - mid
        o_ref[hd] = jnp.where(lane == 0, hi, jnp.where(lane == 1, mid, jnp.where(
            lane == 2, lo, 0.0))).astype(o_ref.dtype)


def _forget_bias(h, w_f, b_f, batch, seq, heads, tm=512):
    n, d = h.shape
    tm = _tile(seq, tm)
    tps = seq // tm
    return pl.pallas_call(
        functools.partial(_fgate_kernel, tiles_per_seq=tps, heads=heads),
        out_shape=jax.ShapeDtypeStruct((batch * heads, seq, LANES), BF16),
        grid=(n // tm,),
        in_specs=[pl.BlockSpec((tm, d), lambda i: (i, 0)),
                  pl.BlockSpec((d, LANES), lambda i: (0, 0)),
                  pl.BlockSpec((1, LANES), lambda i: (0, 0))],
        out_specs=pl.BlockSpec((heads, tm, LANES), lambda i: (i // tps, i % tps, 0)),
        scratch_shapes=[pltpu.VMEM((1, LANES), F32)],
        compiler_params=_cparams(("arbitrary",)),
        name="forget_bias",
    )(h, w_f, b_f)


def _attn_a_kernel(q_ref, kp_ref, kc_ref, vp_ref, vc_ref, ext_ref, o_ref, bias_ref, *, tq):
    qi = pl.program_id(2)
    n_pairs = tq // A_PAIR

    @pl.when(qi == 0)
    def _():
        r = lax.broadcasted_iota(jnp.int32, (A_PAIR, A_PAIR), 0)
        c = lax.broadcasted_iota(jnp.int32, (A_PAIR, A_PAIR), 1)
        n_seg = A_WIN // A_PAIR + 1
        rolled = [pltpu.roll(jnp.broadcast_to(
            ext_ref[0, :, j * A_PAIR:(j + 1) * A_PAIR], (A_PAIR, A_PAIR)), 0, 1,
            stride=1, stride_axis=0) for j in range(n_seg)]
        lo = (r // CHUNK) * CHUNK
        for j in range(n_seg - 1):
            tile = jnp.where(c >= r, rolled[j + 1], rolled[j])
            cc = c + j * A_PAIR
            band = jnp.logical_and(cc >= lo, cc < lo + (LEFT_CHUNKS + 1) * CHUNK)
            bias_ref[:, j * A_PAIR:(j + 1) * A_PAIR] = jnp.where(band, tile, NEG_BIG)

    left = LEFT_CHUNKS * CHUNK
    ones = jnp.ones((A_WIN, HEAD_DIM), BF16)

    def run(first_block):
        bias = bias_ref[...]
        col = lax.broadcasted_iota(jnp.int32, (A_PAIR, A_WIN), 1)
        for p in range(n_pairs):
            ws = p * A_PAIR
            q = q_ref[ws:ws + A_PAIR, :]
            if ws < left:
                k = jnp.concatenate([kp_ref[ws:left, :], kc_ref[0:ws + A_PAIR, :]], axis=0)
                v = jnp.concatenate([vp_ref[ws:left, :], vc_ref[0:ws + A_PAIR, :]], axis=0)
            else:
                k = kc_ref[ws - left:ws + A_PAIR, :]
                v = vc_ref[ws - left:ws + A_PAIR, :]
            s = lax.dot_general(q, k, (((1,), (1,)), ((), ())),
                                preferred_element_type=F32) + bias
            if first_block and ws < left:
                s = jnp.where(col >= left - ws, s, NEG_BIG)
            m = jnp.max(s, axis=-1, keepdims=True)
            e = jnp.exp2(s - m).astype(BF16)
            o = jnp.dot(e, jnp.concatenate([v, ones], axis=1), preferred_element_type=F32)
            o_ref[ws:ws + A_PAIR, :] = (o[:, :HEAD_DIM] /
                                        o[:, HEAD_DIM:HEAD_DIM + 1]).astype(o_ref.dtype)

    pl.when(qi == 0)(functools.partial(run, True))
    pl.when(qi > 0)(functools.partial(run, False))


def _attention_a(proj, bias_ext, batch, seq, heads, q_col, k_col, v_col, tq=1024):
    left = LEFT_CHUNKS * CHUNK
    tq = _tile(seq, tq)
    assert tq % left == 0 and left % A_PAIR == 0
    nq = seq // tq

    def cur(col0):
        return pl.BlockSpec((tq, HEAD_DIM), lambda b, h, i: (b * nq + i, col0 + h))

    def prev(col0):
        per_seq = seq // left
        return pl.BlockSpec((left, HEAD_DIM), lambda b, h, i: (
            b * per_seq + jnp.maximum(i * (tq // left) - 1, 0), col0 + h))

    return pl.pallas_call(
        functools.partial(_attn_a_kernel, tq=tq),
        out_shape=jax.ShapeDtypeStruct((batch * seq, heads * HEAD_DIM), BF16),
        grid=(batch, heads, nq),
        in_specs=[cur(q_col), prev(k_col), cur(k_col), prev(v_col), cur(v_col),
                  pl.BlockSpec((1, 1, A_WIN + A_PAIR), lambda b, h, i: (h, 0, 0))],
        out_specs=pl.BlockSpec((tq, HEAD_DIM), lambda b, h, i: (b * nq + i, h)),
        scratch_shapes=[pltpu.VMEM((A_PAIR, A_WIN), F32)],
        compiler_params=_cparams(("parallel", "parallel", "arbitrary")),
        name="attention_a",
    )(proj, proj, proj, proj, proj, bias_ext)


def _attn_a_bias_vector(rel_bias):
    x = jnp.arange(A_WIN + A_PAIR)
    idx = jnp.clip(LEFT_CHUNKS * CHUNK + A_PAIR - x, -REL_CLIP, REL_CLIP) + REL_CLIP
    ext = rel_bias.astype(F32)[:, idx] * LOG2E
    return ext.reshape(rel_bias.shape[0], 1, A_WIN + A_PAIR)


def _attn_b_kernel(q_ref, k_ref, v_ref, aug_ref, o_ref, acc_ref, s_ref, m_ref, *, tq, sub):
    qi = pl.program_id(2)
    n_sub = tq // sub
    nt = (((1,), (1,)), ((), ()))
    lane = lax.broadcasted_iota(jnp.int32, (tq, HEAD_DIM), 1)
    qx = jnp.concatenate([q_ref[...], jnp.where(lane < B_AUG, 1.0, 0.0).astype(BF16)],
                         axis=1)
    ones = jnp.ones((tq, HEAD_DIM), BF16)

    def key_block(ki):
        rows = pl.ds(pl.multiple_of(ki * tq, tq), tq)
        return jnp.concatenate([k_ref[rows, :], aug_ref[0, rows, :]], axis=1)

    def value_block(ki):
        rows = pl.ds(pl.multiple_of(ki * tq, tq), tq)
        return jnp.concatenate([v_ref[rows, :], ones], axis=1)

    def absorb(r, vx, n_keys, masked):
        rows = slice(r * sub, (r + 1) * sub)
        s = s_ref[rows, :n_keys]
        if masked:
            row = lax.broadcasted_iota(jnp.int32, (sub, n_keys), 0) + r * sub
            col = lax.broadcasted_iota(jnp.int32, (sub, n_keys), 1)
            s = jnp.where(col <= row, s, NEG_BIG)
        m_old = m_ref[rows, :]
        m_new = jnp.maximum(m_old, jnp.max(s, axis=-1, keepdims=True))
        e = jnp.exp2(s - m_new).astype(BF16)
        acc_ref[rows, :] = jnp.exp2(m_old - m_new) * acc_ref[rows, :] + jnp.dot(
            e, vx[:n_keys], preferred_element_type=F32)
        m_ref[rows, :] = m_new

    acc_ref[...] = jnp.zeros_like(acc_ref)
    m_ref[...] = jnp.full_like(m_ref, -jnp.inf)
    s_ref[...] = lax.dot_general(qx, key_block(0), nt, preferred_element_type=F32)

    def body(ki, carry):
        kx = key_block(ki + 1)
        vx = value_block(ki)
        for r in range(n_sub):
            rows = slice(r * sub, (r + 1) * sub)
            s_next = lax.dot_general(qx[rows], kx, nt, preferred_element_type=F32)
            absorb(r, vx, tq, False)
            s_ref[rows, :] = s_next
        return carry

    lax.fori_loop(0, qi, body, 0)
    vx = value_block(qi)
    for r in range(n_sub):
        absorb(r, vx, (r + 1) * sub, True)
    acc = acc_ref[...]
    o_ref[...] = (acc[:, :HEAD_DIM] / acc[:, HEAD_DIM:HEAD_DIM + 1]).astype(o_ref.dtype)


def _attention_b(proj, k_aug, batch, seq, heads, q_col, k_col, v_col, tq=1024, sub=256):
    tq = _tile(seq, tq)
    sub = _tile(tq, sub)
    nq = seq // tq
    return pl.pallas_call(
        functools.partial(_attn_b_kernel, tq=tq, sub=sub),
        out_shape=jax.ShapeDtypeStruct((batch * seq, heads * HEAD_DIM), BF16),
        grid=(batch, heads, nq),
        in_specs=[pl.BlockSpec((tq, HEAD_DIM), lambda b, h, i: (b * nq + i, q_col + h)),
                  pl.BlockSpec((seq, HEAD_DIM), lambda b, h, i: (b, k_col + h)),
                  pl.BlockSpec((seq, HEAD_DIM), lambda b, h, i: (b, v_col + h)),
                  pl.BlockSpec((1, seq, LANES), lambda b, h, i: (b * heads + h, 0, 0))],
        out_specs=pl.BlockSpec((tq, HEAD_DIM), lambda b, h, i: (b * nq + i, h)),
        scratch_shapes=[pltpu.VMEM((tq, 2 * HEAD_DIM), F32), pltpu.VMEM((tq, tq), F32),
                        pltpu.VMEM((tq, 1), F32)],
        compiler_params=_cparams(("parallel", "parallel", "arbitrary")),
        name="attention_b",
    )(proj, proj, proj, k_aug)


def _ssm_operators(are_ref, aim_ref, ldt_ref, btr_ref, bti_ref, cr_ref, ci_ref,
                   kmat_ref, wst_ref, cpw_ref, ktab_ref):
    t_len, n_st, n_ch = SSM_T, SSM_STATE, SSM_GROUP
    lr = jnp.minimum(are_ref[0], -1e-4)
    li = aim_ref[0]
    dt = jnp.exp(ldt_ref[0])
    xr, xi = lr * dt, li * dt

    def power(tau):
        mag = jnp.exp(tau * xr)
        return mag * jnp.cos(tau * xi), mag * jnp.sin(tau * xi)

    one = jnp.ones((1, 1), F32)
    er, ei = power(one)
    den = lr * lr + li * li
    nr, ni = er - 1.0, ei
    cfr = (nr * lr + ni * li) / den
    cfi = (ni * lr - nr * li) / den
    btr, bti = btr_ref[0], bti_ref[0]
    bbr = cfr * btr - cfi * bti
    bbi = cfr * bti + cfi * btr
    cr, ci = cr_ref[0], ci_ref[0]

    tau = lax.broadcasted_iota(jnp.int32, (t_len, 2 * n_st), 0).astype(F32)
    lane = lax.broadcasted_iota(jnp.int32, (t_len, 2 * n_st), 1)
    tau = jnp.where(lane < n_st, tau, float(t_len - 1) - tau)
    mag = jnp.exp(tau * jnp.concatenate([xr, xr], axis=1))
    ang = tau * jnp.concatenate([xi, xi], axis=1)
    pw_r, pw_i = mag * jnp.cos(ang), mag * jnp.sin(ang)
    pr, pi = pw_r[:, :n_st], pw_i[:, :n_st]
    rr, ri = pw_r[:, n_st:], pw_i[:, n_st:]
    qr, qi = pr * er - pi * ei, pr * ei + pi * er

    w_rows = []
    for ip in range(n_ch):
        c_r, c_i = cr[ip:ip + 1, :], ci[ip:ip + 1, :]
        w_rows.append(jnp.concatenate([c_r * bbr - c_i * bbi,
                                       -(c_r * bbi + c_i * bbr)], axis=1))
    p_cat = jnp.concatenate([pr, pi], axis=1)
    halves = []
    for par in range(2):
        w_par = jnp.concatenate(w_rows[par::2], axis=0)
        halves.append(lax.dot_general(w_par, p_cat, (((1,), (1,)), ((), ())),
                                      preferred_element_type=F32,
                                      precision=lax.Precision.HIGHEST))
    ktab_ref[...] = jnp.concatenate(halves, axis=1)

    row = lax.broadcasted_iota(jnp.int32, (t_len, 2 * t_len), 0)
    col = lax.broadcasted_iota(jnp.int32, (t_len, 2 * t_len), 1)
    causal = col % t_len >= row

    def fill(i, carry):
        r0 = pl.multiple_of(i * t_len, t_len)
        for m in range(n_ch // 2):
            kvec = ktab_ref[pl.ds(m * n_ch + i, 1), :]
            blk = pltpu.roll(jnp.broadcast_to(kvec, (t_len, 2 * t_len)), 0, 1,
                             stride=1, stride_axis=0)
            kmat_ref[pl.ds(r0, t_len), 2 * m * t_len:2 * (m + 1) * t_len] = (
                jnp.where(causal, blk, 0.0).astype(kmat_ref.dtype))
        return carry

    lax.fori_loop(0, n_ch, fill, 0)

    for i in range(n_ch):
        br, bi = bbr[i:i + 1, :], bbi[i:i + 1, :]
        wst_ref[i * t_len:(i + 1) * t_len, :] = jnp.concatenate(
            [rr * br - ri * bi, rr * bi + ri * br], axis=1).astype(wst_ref.dtype)
        c_r, c_i = cr[i:i + 1, :], ci[i:i + 1, :]
        cpw_ref[i * t_len:(i + 1) * t_len, :] = jnp.concatenate(
            [c_r * qr - c_i * qi, -(c_r * qi + c_i * qr)], axis=1).astype(cpw_ref.dtype)

    kk = lax.broadcasted_iota(jnp.int32, (8, 1), 0)
    ar, ai = power(float(t_len) * jnp.exp2(kk.astype(F32)))
    return jnp.concatenate([ar, ar], axis=1), jnp.concatenate([-ai, ai], axis=1)


def _gelu_tanh(x):
    return 0.5 * x * (1.0 + jnp.tanh(math.sqrt(2.0 / math.pi) * (x + 0.044715 * x * x * x)))


def _ssm_kernel(d_ref, are_ref, aim_ref, ldt_ref, btr_ref, bti_ref, cr_ref, ci_ref, u_ref,
                o_ref, kmat_ref, wst_ref, cpw_ref, ktab_ref, *, rows_per_seq):
    g = pl.program_id(0)
    n_ch, t_len, n2 = SSM_GROUP, SSM_T, 2 * SSM_STATE
    a1, a2 = _ssm_operators(are_ref, aim_ref, ldt_ref, btr_ref, bti_ref, cr_ref, ci_ref,
                            kmat_ref, wst_ref, cpw_ref, ktab_ref)
    n_rows = u_ref.shape[1]
    lane_u = lax.broadcasted_iota(jnp.int32, (n_rows, LANES), 1)
    first = lane_u < t_len
    ev, od = [], []
    for k in range(n_ch // 2):
        ua, ub = u_ref[2 * k], u_ref[2 * k + 1]
        ev.append(jnp.where(first, ua, pltpu.roll(ub, t_len, 1)))
        od.append(jnp.where(first, pltpu.roll(ua, t_len, 1), ub))
    x = jnp.concatenate([jnp.concatenate(ev, axis=1), jnp.concatenate(od, axis=1)],
                        axis=0)
    y = jnp.dot(x, kmat_ref[...], preferred_element_type=F32)
    e = jnp.dot(x, wst_ref[...], preferred_element_type=F32)

    def cmul(k, z):
        return a1[k:k + 1, :] * z + a2[k:k + 1, :] * pltpu.roll(z, SSM_STATE, 1)

    e_even, e_odd = e[:n_rows], e[n_rows:]
    f = cmul(0, e_even) + e_odd
    pos = lax.broadcasted_iota(jnp.int32, (n_rows, n2), 0) % rows_per_seq
    step, k = 1, 1
    while step < rows_per_seq:
        f = f + cmul(k, jnp.where(pos >= step, pltpu.roll(f, step, 0), 0.0))
        step, k = step * 2, k + 1
    s_even = jnp.where(pos >= 1, pltpu.roll(f, 1, 0), 0.0)
    s_odd = cmul(0, s_even) + e_even
    s_start = jnp.concatenate([s_even, s_odd], axis=0).astype(BF16)
    y = y + lax.dot_general(s_start, cpw_ref[...], (((1,), (1,)), ((), ())),
                            preferred_element_type=F32)
    y_even, y_odd = y[:n_rows], y[n_rows:]
    for k in range(n_ch // 2):
        ye = y_even[:, 2 * k * t_len:2 * (k + 1) * t_len]
        yo = y_odd[:, 2 * k * t_len:2 * (k + 1) * t_len]
        ya = jnp.where(first, ye, pltpu.roll(yo, t_len, 1))
        yb = jnp.where(first, pltpu.roll(ye, t_len, 1), yo)
        for i, yi in ((2 * k, ya), (2 * k + 1, yb)):
            yi = yi + d_ref[g * n_ch + i] * u_ref[i].astype(F32)
            o_ref[i] = _gelu_tanh(yi).astype(o_ref.dtype)


def _ssm(u_t, d_skip, a_re, a_im, log_dt, b_re, b_im, c_re, c_im, seq):
    w, n = u_t.shape
    g, n_st = a_re.shape
    n_ch, t_len = SSM_GROUP, SSM_T
    c = n // LANES
    assert g * n_ch == w and 2 * t_len == LANES and seq % LANES == 0 and seq // LANES <= 2 ** 7
    gw = n_ch * t_len
    u3 = u_t.reshape(w, c, LANES)
    vec = lambda a: a.reshape(g, 1, -1).astype(F32)
    bt = lambda a: jnp.swapaxes(a, 1, 2).astype(F32)
    spec3 = lambda s1, s2: pl.BlockSpec((1, s1, s2), lambda i: (i, 0, 0))
    out = pl.pallas_call(
        functools.partial(_ssm_kernel, rows_per_seq=seq // LANES),
        out_shape=jax.ShapeDtypeStruct((w, c, LANES), BF16),
        grid=(g,),
        in_specs=[pl.BlockSpec(memory_space=pltpu.SMEM),
                  spec3(1, n_st), spec3(1, n_st), spec3(1, 1),
                  spec3(n_ch, n_st), spec3(n_ch, n_st),
                  spec3(n_ch, n_st), spec3(n_ch, n_st),
                  pl.BlockSpec((n_ch, c, LANES), lambda i: (i, 0, 0))],
        out_specs=pl.BlockSpec((n_ch, c, LANES), lambda i: (i, 0, 0)),
        scratch_shapes=[pltpu.VMEM((gw, gw), BF16), pltpu.VMEM((gw, 2 * n_st), BF16),
                        pltpu.VMEM((gw, 2 * n_st), BF16),
                        pltpu.VMEM((n_ch * n_ch // 2, 2 * t_len), F32)],
        compiler_params=_cparams(("parallel",)),
        name="ssm",
    )(d_skip.astype(F32), vec(a_re), vec(a_im), log_dt.reshape(g, 1, 1).astype(F32),
      bt(b_re), bt(b_im), c_re.astype(F32), c_im.astype(F32), u3)
    return out.reshape(w, n)


def _glu_kernel(w_ref, y_ref, b_ref, o_ref, *, tr):
    j = pl.program_id(1)
    z = jnp.dot(w_ref[...], y_ref[...], preferred_element_type=F32) + b_ref[...]
    rows = pl.ds(pl.multiple_of(j * tr, tr), tr)
    o_ref[...] = (y_ref[rows, :].astype(F32) * jax.nn.sigmoid(z)).astype(o_ref.dtype)


def _glu(y_t, w_glu_t, layer, b_glu, tr=1024, tn=1024):
    w, n = y_t.shape
    tr, tn = _tile(w, tr), _tile(n, tn)
    return pl.pallas_call(
        functools.partial(_glu_kernel, tr=tr),
        out_shape=jax.ShapeDtypeStruct((w, n), BF16),
        grid=(n // tn, w // tr),
        in_specs=[pl.BlockSpec((None, tr, w), lambda i, j: (layer, j, 0)),
                  pl.BlockSpec((w, tn), lambda i, j: (0, i)),
                  pl.BlockSpec((tr, 1), lambda i, j: (j, 0))],
        out_specs=pl.BlockSpec((tr, tn), lambda i, j: (j, i)),
        compiler_params=_cparams(("parallel", "arbitrary")),
        name="glu",
    )(w_glu_t, y_t, b_glu.reshape(w, 1).astype(F32))


def _outproj_kernel(oa_ref, ob_ref, oc_ref, ga_ref, gb_ref, gc_ref,
                    wa_ref, wb_ref, wc_ref, o_ref, na_ref, nb_ref, nc_ref):
    @pl.when(pl.program_id(1) == 0)
    def _():
        na_ref[...] = _rms_rows(oa_ref[...].astype(F32), ga_ref[...]).astype(BF16)
        nb_ref[...] = _rms_rows(ob_ref[...].astype(F32), gb_ref[...]).astype(BF16)
        oc = oc_ref[...].astype(F32)
        inv = lax.rsqrt(jnp.mean(oc * oc, axis=0, keepdims=True) + RMS_EPS)
        nc_ref[...] = (oc * inv * gc_ref[...]).astype(BF16)

    acc = jnp.dot(na_ref[...], wa_ref[...], preferred_element_type=F32)
    acc += jnp.dot(nb_ref[...], wb_ref[...], preferred_element_type=F32)
    acc += lax.dot_general(nc_ref[...], wc_ref[...], (((0,), (0,)), ((), ())),
                           preferred_element_type=F32)
    o_ref[...] = acc.astype(o_ref.dtype)


def _outproj(o_a, o_b, o_c_t, g_group, w_out, layer, tm=512, tn=1024):
    n, wa = o_a.shape
    wb = o_b.shape[1]
    wc = o_c_t.shape[0]
    d = w_out.shape[2]
    tm, tn = _tile(n, tm), _tile(d, tn)
    assert wa % wb == 0 and (wa + wb) % wc == 0
    g = g_group.astype(F32)
    return pl.pallas_call(
        _outproj_kernel,
        out_shape=jax.ShapeDtypeStruct((n, d), BF16),
        grid=(n // tm, d // tn),
        in_specs=[pl.BlockSpec((tm, wa), lambda i, j: (i, 0)),
                  pl.BlockSpec((tm, wb), lambda i, j: (i, 0)),
                  pl.BlockSpec((wc, tm), lambda i, j: (0, i)),
                  pl.BlockSpec((1, wa), lambda i, j: (0, 0)),
                  pl.BlockSpec((1, wb), lambda i, j: (0, 0)),
                  pl.BlockSpec((wc, 1), lambda i, j: (0, 0)),
                  pl.BlockSpec((None, wa, tn), lambda i, j: (layer, 0, j)),
                  pl.BlockSpec((None, wb, tn), lambda i, j: (layer, wa // wb, j)),
                  pl.BlockSpec((None, wc, tn), lambda i, j: (layer, (wa + wb) // wc, j))],
        out_specs=pl.BlockSpec((tm, tn), lambda i, j: (i, j)),
        scratch_shapes=[pltpu.VMEM((tm, wa), BF16), pltpu.VMEM((tm, wb), BF16),
                        pltpu.VMEM((wc, tm), BF16)],
        compiler_params=_cparams(("parallel", "arbitrary")),
        name="outproj",
    )(o_a, o_b, o_c_t, g[:wa].reshape(1, wa), g[wa:wa + wb].reshape(1, wb),
      g[wa + wb:].reshape(wc, 1), w_out, w_out, w_out)


def _residual_kernel(y_ref, x_ref, gp_ref, gn_ref, xo_ref, ho_ref):
    xn = x_ref[...] + _rms_rows(y_ref[...].astype(F32), gp_ref[...])
    xo_ref[...] = xn
    ho_ref[...] = _rms_rows(xn, gn_ref[...]).astype(ho_ref.dtype)


def _residual_last_kernel(y_ref, x_ref, gp_ref, xo_ref):
    xo_ref[...] = x_ref[...] + _rms_rows(y_ref[...].astype(F32), gp_ref[...])


def _residual_norm(y, x, g_post, g_next, tm=256):
    n, d = x.shape
    tm = _tile(n, tm)
    row = pl.BlockSpec((tm, d), lambda i: (i, 0))
    vec = pl.BlockSpec((1, d), lambda i: (0, 0))
    gp = g_post.reshape(1, d).astype(F32)
    if g_next is None:
        return pl.pallas_call(
            _residual_last_kernel,
            out_shape=jax.ShapeDtypeStruct((n, d), F32),
            grid=(n // tm,), in_specs=[row, row, vec], out_specs=row,
            compiler_params=_cparams(("parallel",)), name="residual_last",
        )(y, x, gp), None
    return pl.pallas_call(
        _residual_kernel,
        out_shape=(jax.ShapeDtypeStruct((n, d), F32), jax.ShapeDtypeStruct((n, d), BF16)),
        grid=(n // tm,), in_specs=[row, row, vec, vec], out_specs=(row, row),
        compiler_params=_cparams(("parallel",)), name="residual_norm",
    )(y, x, gp, g_next.reshape(1, d).astype(F32))


def _ffn_kernel(h_ref, wg_ref, wu_ref, wd_ref, o_ref, acc_ref):
    f = pl.program_id(1)

    @pl.when(f == 0)
    def _():
        acc_ref[...] = jnp.zeros_like(acc_ref)

    h = h_ref[...]
    gate = jnp.dot(h, wg_ref[...], preferred_element_type=F32)
    up = jnp.dot(h, wu_ref[...], preferred_element_type=F32)
    act = (gate * jax.nn.sigmoid(gate) * up).astype(BF16)
    acc_ref[...] += jnp.dot(act, wd_ref[...], preferred_element_type=F32)

    @pl.when(f == pl.num_programs(1) - 1)
    def _():
        o_ref[...] = acc_ref[...].astype(o_ref.dtype)


def _ffn(h, w_gate, w_up, w_down, layer, tm=1024, tf=256):
    n, d = h.shape
    d_ff = w_gate.shape[2]
    tm = _tile(n, tm)
    assert d_ff % tf == 0
    once = pl.Buffered(1)
    return pl.pallas_call(
        _ffn_kernel,
        out_shape=jax.ShapeDtypeStruct((n, d), BF16),
        grid=(n // tm, d_ff // tf),
        in_specs=[pl.BlockSpec((tm, d), lambda i, f: (i, 0), pipeline_mode=once),
                  pl.BlockSpec((None, d, tf), lambda i, f: (layer, 0, f)),
                  pl.BlockSpec((None, d, tf), lambda i, f: (layer, 0, f)),
                  pl.BlockSpec((None, tf, d), lambda i, f: (layer, f, 0))],
        out_specs=pl.BlockSpec((tm, d), lambda i, f: (i, 0), pipeline_mode=once),
        scratch_shapes=[pltpu.VMEM((tm, d), F32)],
        compiler_params=_cparams(("parallel", "arbitrary")),
        name="ffn",
    )(h, w_gate, w_up, w_down)


def kernel(x, w_in, w_out, g_pre_mix, g_post_mix, g_group, rel_bias, b_forget,
           ssm_a_re, ssm_a_im, ssm_log_dt, ssm_b_re, ssm_b_im, ssm_c_re, ssm_c_im,
           ssm_d, w_glu, b_glu, g_pre_ffn, g_post_ffn, w_ffn_gate, w_ffn_up, w_ffn_down):
    batch, seq, d_model = x.shape
    depth = w_in.shape[0]
    a_heads = rel_bias.shape[1]
    b_heads = b_forget.shape[1]
    a_w, b_w = a_heads * HEAD_DIM, b_heads * HEAD_DIM
    ssm_w = ssm_d.shape[1]
    n_tok = batch * seq
    tok_cols = 3 * a_w + 3 * b_w
    assert w_in.shape[2] == tok_cols + b_heads + ssm_w
    assert b_heads <= LANES
    scale = HEAD_DIM ** -0.5

    w_in = w_in.astype(F32)
    col_scale = jnp.concatenate([
        jnp.full((a_w,), scale * LOG2E, F32), jnp.ones((2 * a_w,), F32),
        jnp.full((b_w,), scale * LOG2E, F32), jnp.ones((2 * b_w,), F32)])
    w_f = jnp.pad(w_in[:, :, tok_cols:tok_cols + b_heads],
                  ((0, 0), (0, 0), (0, LANES - b_heads))).astype(BF16)
    b_f = jnp.pad(b_forget.astype(F32), ((0, 0), (0, LANES - b_heads)))
    w_uc_t = _uc_weight_t(w_in, tok_cols, b_heads, ssm_w)
    w_glu_t = jnp.swapaxes(w_glu, 1, 2).astype(BF16)
    w_out_b = w_out.astype(BF16)
    w_gate_b, w_up_b, w_down_b = (w.astype(BF16) for w in (w_ffn_gate, w_ffn_up, w_ffn_down))

    xf = x.reshape(n_tok, d_model).astype(F32)
    h = _rmsnorm(xf, g_pre_mix[0].astype(F32))
    for l in range(depth):
        proj = _inproj_tok(h, w_in, l, col_scale, tok_cols)
        u_t = _matmul_nt(w_uc_t, l, h, name="inproj_ssm")
        k_aug = _forget_bias(h, w_f[l], b_f[l:l + 1], batch, seq, b_heads)

        o_a = _attention_a(proj, _attn_a_bias_vector(rel_bias[l]), batch, seq, a_heads,
                           0, a_heads, 2 * a_heads)
        c0 = 3 * a_heads
        o_b = _attention_b(proj, k_aug, batch, seq, b_heads,
                           c0, c0 + b_heads, c0 + 2 * b_heads)
        y_t = _ssm(u_t, ssm_d[l], ssm_a_re[l], ssm_a_im[l], ssm_log_dt[l],
                   ssm_b_re[l], ssm_b_im[l], ssm_c_re[l], ssm_c_im[l], seq)
        o_c_t = _glu(y_t, w_glu_t, l, b_glu[l])

        mix = _outproj(o_a, o_b, o_c_t, g_group[l], w_out_b, l)
        xf, h2 = _residual_norm(mix, xf, g_post_mix[l], g_pre_ffn[l])

        f = _ffn(h2, w_gate_b, w_up_b, w_down_b, l)
        g_next = g_pre_mix[l + 1] if l + 1 < depth else None
        xf, h = _residual_norm(f, xf, g_post_ffn[l], g_next)
    return xf.reshape(batch, seq, d_model).astype(x.dtype)
```

```python
import functools
import math

import jax
import jax.numpy as jnp
from jax import lax
from jax.experimental import pallas as pl
from jax.experimental.pallas import tpu as pltpu

CHUNK = 64
LEFT_CHUNKS = 8
REL_CLIP = 256
HEAD_DIM = 128
SSM_GROUP = 16
SSM_STATE = 64
RMS_EPS = 1e-6

LANES = 128
VMEM_LIMIT_BYTES = 56 * 1024 * 1024

SSM_T = 64
A_PAIR = 2 * CHUNK
A_WIN = A_PAIR + LEFT_CHUNKS * CHUNK
NEG_BIG = -1e30
LOG2E = math.log2(math.e)
B_AUG = 3

F32 = jnp.float32
BF16 = jnp.bfloat16


def _cparams(sem):
    return pltpu.CompilerParams(dimension_semantics=sem,
                                vmem_limit_bytes=VMEM_LIMIT_BYTES)


def _tile(n, want):
    t = min(n, want)
    while n % t:
        t //= 2
    return t


def _rms_rows(xf, g):
    return xf * lax.rsqrt(jnp.mean(xf * xf, axis=-1, keepdims=True) + RMS_EPS) * g


def _rmsnorm_kernel(x_ref, g_ref, o_ref):
    o_ref[...] = _rms_rows(x_ref[...], g_ref[...]).astype(o_ref.dtype)


def _rmsnorm(x, g, tm=256):
    n, d = x.shape
    tm = _tile(n, tm)
    return pl.pallas_call(
        _rmsnorm_kernel,
        out_shape=jax.ShapeDtypeStruct((n, d), BF16),
        grid=(n // tm,),
        in_specs=[pl.BlockSpec((tm, d), lambda i: (i, 0)),
                  pl.BlockSpec((1, d), lambda i: (0, 0))],
        out_specs=pl.BlockSpec((tm, d), lambda i: (i, 0)),
        compiler_params=_cparams(("parallel",)),
        name="rmsnorm",
    )(x, g.reshape(1, d))


_NT = (((1,), (1,)), ((), ()))


def _inproj_tok_kernel(a_ref, w_ref, s_ref, o_ref, wb_ref):
    @pl.when(pl.program_id(1) == 0)
    def _():
        wb_ref[...] = (w_ref[...] * s_ref[...]).astype(BF16)

    o_ref[...] = lax.dot_general(a_ref[...], wb_ref[...], _NT,
                                 preferred_element_type=F32).astype(o_ref.dtype)


def _inproj_tok(a, w_t, layer, col_scale, n_cols, tm=1024, tn=512):
    m, k = a.shape
    tm, tn = _tile(m, tm), _tile(n_cols, tn)
    return pl.pallas_call(
        _inproj_tok_kernel,
        out_shape=jax.ShapeDtypeStruct((m, n_cols), BF16),
        grid=(n_cols // tn, m // tm),
        in_specs=[pl.BlockSpec((tm, k), lambda j, i: (i, 0)),
                  pl.BlockSpec((None, tn, k), lambda j, i: (layer, j, 0)),
                  pl.BlockSpec((tn, 1), lambda j, i: (j, 0))],
        out_specs=pl.BlockSpec((tm, tn), lambda j, i: (i, j)),
        scratch_shapes=[pltpu.VMEM((tn, k), BF16)],
        compiler_params=_cparams(("parallel", "arbitrary")),
        name="inproj_tok",
    )(a, w_t, col_scale.reshape(n_cols, 1))


def _inproj_ssm_kernel(wa_ref, wn_ref, a_ref, o_ref, wb_ref, *, shift):
    @pl.when(pl.program_id(1) == 0)
    def _():
        wb_ref[...] = jnp.concatenate([wa_ref[shift:, :], wn_ref[:shift, :]],
                                      axis=0).astype(BF16)

    o_ref[...] = lax.dot_general(wb_ref[...], a_ref[...], _NT,
                                 preferred_element_type=F32).astype(o_ref.dtype)


def _inproj_ssm(w_t, layer, row0, shift, n_rows, a, tr=512, tn=1024):
    m, k = a.shape
    tr, tn = _tile(n_rows, tr), _tile(m, tn)
    nxt = 8 * pl.cdiv(shift, 8)
    assert row0 % tr == 0 and tr % nxt == 0 and 0 < shift
    return pl.pallas_call(
        functools.partial(_inproj_ssm_kernel, shift=shift),
        out_shape=jax.ShapeDtypeStruct((n_rows, m), BF16),
        grid=(n_rows // tr, m // tn),
        in_specs=[pl.BlockSpec((None, tr, k), lambda j, i: (layer, row0 // tr + j, 0)),
                  pl.BlockSpec((None, nxt, k),
                               lambda j, i: (layer, (row0 + (j + 1) * tr) // nxt, 0)),
                  pl.BlockSpec((tn, k), lambda j, i: (i, 0))],
        out_specs=pl.BlockSpec((tr, tn), lambda j, i: (j, i)),
        scratch_shapes=[pltpu.VMEM((tr, k), BF16)],
        compiler_params=_cparams(("parallel", "arbitrary")),
        name="inproj_ssm",
    )(w_t, w_t, a)


def _fgate_kernel(h_ref, w_ref, b_ref, o_ref, carry_ref, *, tiles_per_seq, heads):
    i = pl.program_id(0)

    @pl.when(i % tiles_per_seq == 0)
    def _():
        carry_ref[...] = jnp.zeros_like(carry_ref)

    logit = lax.dot_general(h_ref[...], w_ref[...], _NT,
                            preferred_element_type=F32) + b_ref[...]
    ls = -(jnp.maximum(-logit, 0.0) + jnp.log1p(jnp.exp(-jnp.abs(logit))))
    tm = ls.shape[0]
    row = lax.broadcasted_iota(jnp.int32, (tm, tm), 0)
    col = lax.broadcasted_iota(jnp.int32, (tm, tm), 1)
    tri = (col <= row).astype(F32)
    cs = jnp.dot(tri, ls, preferred_element_type=F32,
                 precision=lax.Precision.HIGHEST) + carry_ref[...]
    carry_ref[...] = cs[tm - 1:tm, :]
    lane = lax.broadcasted_iota(jnp.int32, (tm, LANES), 1)
    for hd in range(heads):
        x = cs[:, hd:hd + 1] * (-LOG2E)
        hi = x.astype(BF16).astype(F32)
        mid = (x - hi).astype(BF16).astype(F32)
        lo = x - hi - mid
        o_ref[hd] = jnp.where(lane == 0, hi, jnp.where(lane == 1, mid, jnp.where(
            lane == 2, lo, 0.0))).astype(o_ref.dtype)


def _forget_bias(h, w_f, b_f, batch, seq, heads, tm=512):
    n, d = h.shape
    tm = _tile(seq, tm)
    tps = seq // tm
    return pl.pallas_call(
        functools.partial(_fgate_kernel, tiles_per_seq=tps, heads=heads),
        out_shape=jax.ShapeDtypeStruct((batch * heads, seq, LANES), BF16),
        grid=(n // tm,),
        in_specs=[pl.BlockSpec((tm, d), lambda i: (i, 0)),
                  pl.BlockSpec((LANES, d), lambda i: (0, 0)),
                  pl.BlockSpec((1, LANES), lambda i: (0, 0))],
        out_specs=pl.BlockSpec((heads, tm, LANES), lambda i: (i // tps, i % tps, 0)),
        scratch_shapes=[pltpu.VMEM((1, LANES), F32)],
        compiler_params=_cparams(("arbitrary",)),
        name="forget_bias",
    )(h, w_f, b_f)


def _attn_a_kernel(q_ref, kp_ref, kc_ref, vp_ref, vc_ref, ext_ref, o_ref, bias_ref, *, tq):
    qi = pl.program_id(2)
    n_pairs = tq // A_PAIR

    @pl.when(qi == 0)
    def _():
        r = lax.broadcasted_iota(jnp.int32, (A_PAIR, A_PAIR), 0)
        c = lax.broadcasted_iota(jnp.int32, (A_PAIR, A_PAIR), 1)
        n_seg = A_WIN // A_PAIR + 1
        rolled = [pltpu.roll(jnp.broadcast_to(
            ext_ref[0, :, j * A_PAIR:(j + 1) * A_PAIR], (A_PAIR, A_PAIR)), 0, 1,
            stride=1, stride_axis=0) for j in range(n_seg)]
        lo = (r // CHUNK) * CHUNK
        for j in range(n_seg - 1):
            tile = jnp.where(c >= r, rolled[j + 1], rolled[j])
            cc = c + j * A_PAIR
            band = jnp.logical_and(cc >= lo, cc < lo + (LEFT_CHUNKS + 1) * CHUNK)
            bias_ref[:, j * A_PAIR:(j + 1) * A_PAIR] = jnp.where(band, tile, NEG_BIG)

    left = LEFT_CHUNKS * CHUNK
    ones = jnp.ones((A_WIN, HEAD_DIM), BF16)

    def run(first_block):
        bias = bias_ref[...]
        col = lax.broadcasted_iota(jnp.int32, (A_PAIR, A_WIN), 1)
        for p in range(n_pairs):
            ws = p * A_PAIR
            q = q_ref[ws:ws + A_PAIR, :]
            if ws < left:
                k = jnp.concatenate([kp_ref[ws:left, :], kc_ref[0:ws + A_PAIR, :]], axis=0)
                v = jnp.concatenate([vp_ref[ws:left, :], vc_ref[0:ws + A_PAIR, :]], axis=0)
            else:
                k = kc_ref[ws - left:ws + A_PAIR, :]
                v = vc_ref[ws - left:ws + A_PAIR, :]
            s = lax.dot_general(q, k, (((1,), (1,)), ((), ())),
                                preferred_element_type=F32) + bias
            if first_block and ws < left:
                s = jnp.where(col >= left - ws, s, NEG_BIG)
            m = jnp.max(s, axis=-1, keepdims=True)
            e = jnp.exp2(s - m).astype(BF16)
            o = jnp.dot(e, jnp.concatenate([v, ones], axis=1), preferred_element_type=F32)
            o_ref[ws:ws + A_PAIR, :] = (o[:, :HEAD_DIM] /
                                        o[:, HEAD_DIM:HEAD_DIM + 1]).astype(o_ref.dtype)

    pl.when(qi == 0)(functools.partial(run, True))
    pl.when(qi > 0)(functools.partial(run, False))


def _attention_a(proj, bias_ext, batch, seq, heads, q_col, k_col, v_col, tq=1024):
    left = LEFT_CHUNKS * CHUNK
    tq = _tile(seq, tq)
    assert tq % left == 0 and left % A_PAIR == 0
    nq = seq // tq

    def cur(col0):
        return pl.BlockSpec((tq, HEAD_DIM), lambda b, h, i: (b * nq + i, col0 + h))

    def prev(col0):
        per_seq = seq // left
        return pl.BlockSpec((left, HEAD_DIM), lambda b, h, i: (
            b * per_seq + jnp.maximum(i * (tq // left) - 1, 0), col0 + h))

    return pl.pallas_call(
        functools.partial(_attn_a_kernel, tq=tq),
        out_shape=jax.ShapeDtypeStruct((batch * seq, heads * HEAD_DIM), BF16),
        grid=(batch, heads, nq),
        in_specs=[cur(q_col), prev(k_col), cur(k_col), prev(v_col), cur(v_col),
                  pl.BlockSpec((1, 1, A_WIN + A_PAIR), lambda b, h, i: (h, 0, 0))],
        out_specs=pl.BlockSpec((tq, HEAD_DIM), lambda b, h, i: (b * nq + i, h)),
        scratch_shapes=[pltpu.VMEM((A_PAIR, A_WIN), F32)],
        compiler_params=_cparams(("parallel", "parallel", "arbitrary")),
        name="attention_a",
    )(proj, proj, proj, proj, proj, bias_ext)


def _attn_a_bias_vector(rel_bias):
    x = jnp.arange(A_WIN + A_PAIR)
    idx = jnp.clip(LEFT_CHUNKS * CHUNK + A_PAIR - x, -REL_CLIP, REL_CLIP) + REL_CLIP
    ext = rel_bias.astype(F32)[:, idx] * LOG2E
    return ext.reshape(rel_bias.shape[0], 1, A_WIN + A_PAIR)


def _attn_b_kernel(q_ref, k_ref, v_ref, aug_ref, o_ref, acc_ref, s_ref, m_ref, *, tq, sub):
    qi = pl.program_id(2)
    n_sub = tq // sub
    nt = (((1,), (1,)), ((), ()))
    lane = lax.broadcasted_iota(jnp.int32, (tq, HEAD_DIM), 1)
    qx = jnp.concatenate([q_ref[...], jnp.where(lane < B_AUG, 1.0, 0.0).astype(BF16)],
                         axis=1)
    ones = jnp.ones((tq, HEAD_DIM), BF16)

    def key_block(ki):
        rows = pl.ds(pl.multiple_of(ki * tq, tq), tq)
        return jnp.concatenate([k_ref[rows, :], aug_ref[0, rows, :]], axis=1)

    def value_block(ki):
        rows = pl.ds(pl.multiple_of(ki * tq, tq), tq)
        return jnp.concatenate([v_ref[rows, :], ones], axis=1)

    def absorb(r, vx, n_keys, masked):
        rows = slice(r * sub, (r + 1) * sub)
        s = s_ref[rows, :n_keys]
        if masked:
            row = lax.broadcasted_iota(jnp.int32, (sub, n_keys), 0) + r * sub
            col = lax.broadcasted_iota(jnp.int32, (sub, n_keys), 1)
            s = jnp.where(col <= row, s, NEG_BIG)
        m_old = m_ref[rows, :]
        m_new = jnp.maximum(m_old, jnp.max(s, axis=-1, keepdims=True))
        e = jnp.exp2(s - m_new).astype(BF16)
        acc_ref[rows, :] = jnp.exp2(m_old - m_new) * acc_ref[rows, :] + jnp.dot(
            e, vx[:n_keys], preferred_element_type=F32)
        m_ref[rows, :] = m_new

    acc_ref[...] = jnp.zeros_like(acc_ref)
    m_ref[...] = jnp.full_like(m_ref, -jnp.inf)
    s_ref[...] = lax.dot_general(qx, key_block(0), nt, preferred_element_type=F32)

    def body(ki, carry):
        kx = key_block(ki + 1)
        vx = value_block(ki)
        for r in range(n_sub):
            rows = slice(r * sub, (r + 1) * sub)
            s_next = lax.dot_general(qx[rows], kx, nt, preferred_element_type=F32)
            absorb(r, vx, tq, False)
            s_ref[rows, :] = s_next
        return carry

    lax.fori_loop(0, qi, body, 0)
    vx = value_block(qi)
    for r in range(n_sub):
        absorb(r, vx, (r + 1) * sub, True)
    acc = acc_ref[...]
    o_ref[...] = (acc[:, :HEAD_DIM] / acc[:, HEAD_DIM:HEAD_DIM + 1]).astype(o_ref.dtype)


def _attention_b(proj, k_aug, batch, seq, heads, q_col, k_col, v_col, tq=1024, sub=256):
    tq = _tile(seq, tq)
    sub = _tile(tq, sub)
    nq = seq // tq
    return pl.pallas_call(
        functools.partial(_attn_b_kernel, tq=tq, sub=sub),
        out_shape=jax.ShapeDtypeStruct((batch * seq, heads * HEAD_DIM), BF16),
        grid=(batch, heads, nq),
        in_specs=[pl.BlockSpec((tq, HEAD_DIM), lambda b, h, i: (b * nq + i, q_col + h)),
                  pl.BlockSpec((seq, HEAD_DIM), lambda b, h, i: (b, k_col + h)),
                  pl.BlockSpec((seq, HEAD_DIM), lambda b, h, i: (b, v_col + h)),
                  pl.BlockSpec((1, seq, LANES), lambda b, h, i: (b * heads + h, 0, 0))],
        out_specs=pl.BlockSpec((tq, HEAD_DIM), lambda b, h, i: (b * nq + i, h)),
        scratch_shapes=[pltpu.VMEM((tq, 2 * HEAD_DIM), F32), pltpu.VMEM((tq, tq), F32),
                        pltpu.VMEM((tq, 1), F32)],
        compiler_params=_cparams(("parallel", "parallel", "arbitrary")),
        name="attention_b",
    )(proj, proj, proj, k_aug)


def _ssm_operators(are_ref, aim_ref, ldt_ref, btr_ref, bti_ref, cr_ref, ci_ref,
                   kmat_ref, wst_ref, cpw_ref, ktab_ref):
    t_len, n_st, n_ch = SSM_T, SSM_STATE, SSM_GROUP
    lr = jnp.minimum(are_ref[0], -1e-4)
    li = aim_ref[0]
    dt = jnp.exp(ldt_ref[0])
    xr, xi = lr * dt, li * dt

    def power(tau):
        mag = jnp.exp(tau * xr)
        return mag * jnp.cos(tau * xi), mag * jnp.sin(tau * xi)

    one = jnp.ones((1, 1), F32)
    er, ei = power(one)
    den = lr * lr + li * li
    nr, ni = er - 1.0, ei
    cfr = (nr * lr + ni * li) / den
    cfi = (ni * lr - nr * li) / den
    btr, bti = btr_ref[0], bti_ref[0]
    bbr = cfr * btr - cfi * bti
    bbi = cfr * bti + cfi * btr
    cr, ci = cr_ref[0], ci_ref[0]

    tau = lax.broadcasted_iota(jnp.int32, (t_len, 2 * n_st), 0).astype(F32)
    lane = lax.broadcasted_iota(jnp.int32, (t_len, 2 * n_st), 1)
    tau = jnp.where(lane < n_st, tau, float(t_len - 1) - tau)
    mag = jnp.exp(tau * jnp.concatenate([xr, xr], axis=1))
    ang = tau * jnp.concatenate([xi, xi], axis=1)
    pw_r, pw_i = mag * jnp.cos(ang), mag * jnp.sin(ang)
    pr, pi = pw_r[:, :n_st], pw_i[:, :n_st]
    rr, ri = pw_r[:, n_st:], pw_i[:, n_st:]
    qr, qi = pr * er - pi * ei, pr * ei + pi * er

    w_rows = []
    for ip in range(n_ch):
        c_r, c_i = cr[ip:ip + 1, :], ci[ip:ip + 1, :]
        w_rows.append(jnp.concatenate([c_r * bbr - c_i * bbi,
                                       -(c_r * bbi + c_i * bbr)], axis=1))
    p_cat = jnp.concatenate([pr, pi], axis=1)
    halves = []
    for par in range(2):
        w_par = jnp.concatenate(w_rows[par::2], axis=0)
        halves.append(lax.dot_general(w_par, p_cat, (((1,), (1,)), ((), ())),
                                      preferred_element_type=F32,
                                      precision=lax.Precision.HIGHEST))
    ktab_ref[...] = jnp.concatenate(halves, axis=1)

    row = lax.broadcasted_iota(jnp.int32, (t_len, 2 * t_len), 0)
    col = lax.broadcasted_iota(jnp.int32, (t_len, 2 * t_len), 1)
    causal = col % t_len >= row

    def fill(i, carry):
        r0 = pl.multiple_of(i * t_len, t_len)
        for m in range(n_ch // 2):
            kvec = ktab_ref[pl.ds(m * n_ch + i, 1), :]
            blk = pltpu.roll(jnp.broadcast_to(kvec, (t_len, 2 * t_len)), 0, 1,
                             stride=1, stride_axis=0)
            kmat_ref[pl.ds(r0, t_len), 2 * m * t_len:2 * (m + 1) * t_len] = (
                jnp.where(causal, blk, 0.0).astype(kmat_ref.dtype))
        return carry

    lax.fori_loop(0, n_ch, fill, 0)

    for i in range(n_ch):
        br, bi = bbr[i:i + 1, :], bbi[i:i + 1, :]
        wst_ref[i * t_len:(i + 1) * t_len, :] = jnp.concatenate(
            [rr * br - ri * bi, rr * bi + ri * br], axis=1).astype(wst_ref.dtype)
        c_r, c_i = cr[i:i + 1, :], ci[i:i + 1, :]
        cpw_ref[i * t_len:(i + 1) * t_len, :] = jnp.concatenate(
            [c_r * qr - c_i * qi, -(c_r * qi + c_i * qr)], axis=1).astype(cpw_ref.dtype)

    kk = lax.broadcasted_iota(jnp.int32, (8, 1), 0)
    ar, ai = power(float(t_len) * jnp.exp2(kk.astype(F32)))
    return jnp.concatenate([ar, ar], axis=1), jnp.concatenate([-ai, ai], axis=1)


def _gelu_tanh(x):
    return 0.5 * x * (1.0 + jnp.tanh(math.sqrt(2.0 / math.pi) * (x + 0.044715 * x * x * x)))


def _ssm_kernel(d_ref, are_ref, aim_ref, ldt_ref, btr_ref, bti_ref, cr_ref, ci_ref, u_ref,
                o_ref, kmat_ref, wst_ref, cpw_ref, ktab_ref, *, rows_per_seq):
    g = pl.program_id(0)
    n_ch, t_len, n2 = SSM_GROUP, SSM_T, 2 * SSM_STATE
    a1, a2 = _ssm_operators(are_ref, aim_ref, ldt_ref, btr_ref, bti_ref, cr_ref, ci_ref,
                            kmat_ref, wst_ref, cpw_ref, ktab_ref)
    n_rows = u_ref.shape[1]
    lane_u = lax.broadcasted_iota(jnp.int32, (n_rows, LANES), 1)
    first = lane_u < t_len
    ev, od = [], []
    for k in range(n_ch // 2):
        ua, ub = u_ref[2 * k], u_ref[2 * k + 1]
        ev.append(jnp.where(first, ua, pltpu.roll(ub, t_len, 1)))
        od.append(jnp.where(first, pltpu.roll(ua, t_len, 1), ub))
    x = jnp.concatenate([jnp.concatenate(ev, axis=1), jnp.concatenate(od, axis=1)],
                        axis=0)
    y = jnp.dot(x, kmat_ref[...], preferred_element_type=F32)
    e = jnp.dot(x, wst_ref[...], preferred_element_type=F32)

    def cmul(k, z):
        return a1[k:k + 1, :] * z + a2[k:k + 1, :] * pltpu.roll(z, SSM_STATE, 1)

    e_even, e_odd = e[:n_rows], e[n_rows:]
    f = cmul(0, e_even) + e_odd
    pos = lax.broadcasted_iota(jnp.int32, (n_rows, n2), 0) % rows_per_seq
    step, k = 1, 1
    while step < rows_per_seq:
        f = f + cmul(k, jnp.where(pos >= step, pltpu.roll(f, step, 0), 0.0))
        step, k = step * 2, k + 1
    s_even = jnp.where(pos >= 1, pltpu.roll(f, 1, 0), 0.0)
    s_odd = cmul(0, s_even) + e_even
    s_start = jnp.concatenate([s_even, s_odd], axis=0).astype(BF16)
    y = y + lax.dot_general(s_start, cpw_ref[...], (((1,), (1,)), ((), ())),
                            preferred_element_type=F32)
    y_even, y_odd = y[:n_rows], y[n_rows:]
    for k in range(n_ch // 2):
        ye = y_even[:, 2 * k * t_len:2 * (k + 1) * t_len]
        yo = y_odd[:, 2 * k * t_len:2 * (k + 1) * t_len]
        ya = jnp.where(first, ye, pltpu.roll(yo, t_len, 1))
        yb = jnp.where(first, pltpu.roll(ye, t_len, 1), yo)
        for i, yi in ((2 * k, ya), (2 * k + 1, yb)):
            yi = yi + d_ref[g * n_ch + i] * u_ref[i].astype(F32)
            o_ref[i] = _gelu_tanh(yi).astype(o_ref.dtype)


def _ssm(u_t, d_skip, a_re, a_im, log_dt, b_re, b_im, c_re, c_im, seq):
    w, n = u_t.shape
    g, n_st = a_re.shape
    n_ch, t_len = SSM_GROUP, SSM_T
    c = n // LANES
    assert g * n_ch == w and 2 * t_len == LANES and seq % LANES == 0 and seq // LANES <= 2 ** 7
    gw = n_ch * t_len
    u3 = u_t.reshape(w, c, LANES)
    vec = lambda a: a.reshape(g, 1, -1).astype(F32)
    bt = lambda a: jnp.swapaxes(a, 1, 2).astype(F32)
    spec3 = lambda s1, s2: pl.BlockSpec((1, s1, s2), lambda i: (i, 0, 0))
    out = pl.pallas_call(
        functools.partial(_ssm_kernel, rows_per_seq=seq // LANES),
        out_shape=jax.ShapeDtypeStruct((w, c, LANES), BF16),
        grid=(g,),
        in_specs=[pl.BlockSpec(memory_space=pltpu.SMEM),
                  spec3(1, n_st), spec3(1, n_st), spec3(1, 1),
                  spec3(n_ch, n_st), spec3(n_ch, n_st),
                  spec3(n_ch, n_st), spec3(n_ch, n_st),
                  pl.BlockSpec((n_ch, c, LANES), lambda i: (i, 0, 0))],
        out_specs=pl.BlockSpec((n_ch, c, LANES), lambda i: (i, 0, 0)),
        scratch_shapes=[pltpu.VMEM((gw, gw), BF16), pltpu.VMEM((gw, 2 * n_st), BF16),
                        pltpu.VMEM((gw, 2 * n_st), BF16),
                        pltpu.VMEM((n_ch * n_ch // 2, 2 * t_len), F32)],
        compiler_params=_cparams(("parallel",)),
        name="ssm",
    )(d_skip.astype(F32), vec(a_re), vec(a_im), log_dt.reshape(g, 1, 1).astype(F32),
      bt(b_re), bt(b_im), c_re.astype(F32), c_im.astype(F32), u3)
    return out.reshape(w, n)


def _glu_kernel(w_ref, y_ref, b_ref, o_ref, *, tr):
    j = pl.program_id(1)
    z = jnp.dot(w_ref[...], y_ref[...], preferred_element_type=F32) + b_ref[...]
    rows = pl.ds(pl.multiple_of(j * tr, tr), tr)
    o_ref[...] = (y_ref[rows, :].astype(F32) * jax.nn.sigmoid(z)).astype(o_ref.dtype)


def _glu(y_t, w_glu_t, layer, b_glu, tr=1024, tn=1024):
    w, n = y_t.shape
    tr, tn = _tile(w, tr), _tile(n, tn)
    return pl.pallas_call(
        functools.partial(_glu_kernel, tr=tr),
        out_shape=jax.ShapeDtypeStruct((w, n), BF16),
        grid=(n // tn, w // tr),
        in_specs=[pl.BlockSpec((None, tr, w), lambda i, j: (layer, j, 0)),
                  pl.BlockSpec((w, tn), lambda i, j: (0, i)),
                  pl.BlockSpec((tr, 1), lambda i, j: (j, 0))],
        out_specs=pl.BlockSpec((tr, tn), lambda i, j: (j, i)),
        compiler_params=_cparams(("parallel", "arbitrary")),
        name="glu",
    )(w_glu_t, y_t, b_glu.reshape(w, 1).astype(F32))


def _outproj_kernel(oa_ref, ob_ref, oc_ref, ga_ref, gb_ref, gc_ref,
                    wa_ref, wb_ref, wc_ref, o_ref, na_ref, nb_ref, nc_ref):
    @pl.when(pl.program_id(1) == 0)
    def _():
        na_ref[...] = _rms_rows(oa_ref[...].astype(F32), ga_ref[...]).astype(BF16)
        nb_ref[...] = _rms_rows(ob_ref[...].astype(F32), gb_ref[...]).astype(BF16)
        oc = oc_ref[...].astype(F32)
        inv = lax.rsqrt(jnp.mean(oc * oc, axis=0, keepdims=True) + RMS_EPS)
        nc_ref[...] = (oc * inv * gc_ref[...]).astype(BF16)

    acc = jnp.dot(na_ref[...], wa_ref[...], preferred_element_type=F32)
    acc += jnp.dot(nb_ref[...], wb_ref[...], preferred_element_type=F32)
    acc += lax.dot_general(nc_ref[...], wc_ref[...], (((0,), (0,)), ((), ())),
                           preferred_element_type=F32)
    o_ref[...] = acc.astype(o_ref.dtype)


def _outproj(o_a, o_b, o_c_t, g_group, w_out, layer, tm=512, tn=1024):
    n, wa = o_a.shape
    wb = o_b.shape[1]
    wc = o_c_t.shape[0]
    d = w_out.shape[2]
    tm, tn = _tile(n, tm), _tile(d, tn)
    assert wa % wb == 0 and (wa + wb) % wc == 0
    g = g_group.astype(F32)
    return pl.pallas_call(
        _outproj_kernel,
        out_shape=jax.ShapeDtypeStruct((n, d), BF16),
        grid=(n // tm, d // tn),
        in_specs=[pl.BlockSpec((tm, wa), lambda i, j: (i, 0)),
                  pl.BlockSpec((tm, wb), lambda i, j: (i, 0)),
                  pl.BlockSpec((wc, tm), lambda i, j: (0, i)),
                  pl.BlockSpec((1, wa), lambda i, j: (0, 0)),
                  pl.BlockSpec((1, wb), lambda i, j: (0, 0)),
                  pl.BlockSpec((wc, 1), lambda i, j: (0, 0)),
                  pl.BlockSpec((None, wa, tn), lambda i, j: (layer, 0, j)),
                  pl.BlockSpec((None, wb, tn), lambda i, j: (layer, wa // wb, j)),
                  pl.BlockSpec((None, wc, tn), lambda i, j: (layer, (wa + wb) // wc, j))],
        out_specs=pl.BlockSpec((tm, tn), lambda i, j: (i, j)),
        scratch_shapes=[pltpu.VMEM((tm, wa), BF16), pltpu.VMEM((tm, wb), BF16),
                        pltpu.VMEM((wc, tm), BF16)],
        compiler_params=_cparams(("parallel", "arbitrary")),
        name="outproj",
    )(o_a, o_b, o_c_t, g[:wa].reshape(1, wa), g[wa:wa + wb].reshape(1, wb),
      g[wa + wb:].reshape(wc, 1), w_out, w_out, w_out)


def _residual_kernel(y_ref, x_ref, gp_ref, gn_ref, xo_ref, ho_ref):
    xn = x_ref[...] + _rms_rows(y_ref[...].astype(F32), gp_ref[...])
    xo_ref[...] = xn
    ho_ref[...] = _rms_rows(xn, gn_ref[...]).astype(ho_ref.dtype)


def _residual_last_kernel(y_ref, x_ref, gp_ref, xo_ref):
    xo_ref[...] = x_ref[...] + _rms_rows(y_ref[...].astype(F32), gp_ref[...])


def _residual_norm(y, x, g_post, g_next, tm=256):
    n, d = x.shape
    tm = _tile(n, tm)
    row = pl.BlockSpec((tm, d), lambda i: (i, 0))
    vec = pl.BlockSpec((1, d), lambda i: (0, 0))
    gp = g_post.reshape(1, d).astype(F32)
    if g_next is None:
        return pl.pallas_call(
            _residual_last_kernel,
            out_shape=jax.ShapeDtypeStruct((n, d), F32),
            grid=(n // tm,), in_specs=[row, row, vec], out_specs=row,
            compiler_params=_cparams(("parallel",)), name="residual_last",
        )(y, x, gp), None
    return pl.pallas_call(
        _residual_kernel,
        out_shape=(jax.ShapeDtypeStruct((n, d), F32), jax.ShapeDtypeStruct((n, d), BF16)),
        grid=(n // tm,), in_specs=[row, row, vec, vec], out_specs=(row, row),
        compiler_params=_cparams(("parallel",)), name="residual_norm",
    )(y, x, gp, g_next.reshape(1, d).astype(F32))


def _ffn_kernel(h_ref, wg_ref, wu_ref, wd_ref, o_ref, acc_ref):
    f = pl.program_id(1)

    @pl.when(f == 0)
    def _():
        acc_ref[...] = jnp.zeros_like(acc_ref)

    h = h_ref[...]
    gate = jnp.dot(h, wg_ref[...], preferred_element_type=F32)
    up = jnp.dot(h, wu_ref[...], preferred_element_type=F32)
    act = (gate * jax.nn.sigmoid(gate) * up).astype(BF16)
    acc_ref[...] += jnp.dot(act, wd_ref[...], preferred_element_type=F32)

    @pl.when(f == pl.num_programs(1) - 1)
    def _():
        o_ref[...] = acc_ref[...].astype(o_ref.dtype)


def _ffn(h, w_gate, w_up, w_down, layer, tm=1024, tf=256):
    n, d = h.shape
    d_ff = w_gate.shape[2]
    tm = _tile(n, tm)
    assert d_ff % tf == 0
    once = pl.Buffered(1)
    return pl.pallas_call(
        _ffn_kernel,
        out_shape=jax.ShapeDtypeStruct((n, d), BF16),
        grid=(n // tm, d_ff // tf),
        in_specs=[pl.BlockSpec((tm, d), lambda i, f: (i, 0)),
                  pl.BlockSpec((None, d, tf), lambda i, f: (layer, 0, f)),
                  pl.BlockSpec((None, d, tf), lambda i, f: (layer, 0, f)),
                  pl.BlockSpec((None, tf, d), lambda i, f: (layer, f, 0))],
        out_specs=pl.BlockSpec((tm, d), lambda i, f: (i, 0), pipeline_mode=once),
        scratch_shapes=[pltpu.VMEM((tm, d), F32)],
        compiler_params=_cparams(("parallel", "arbitrary")),
        name="ffn",
    )(h, w_gate, w_up, w_down)


def kernel(x, w_in, w_out, g_pre_mix, g_post_mix, g_group, rel_bias, b_forget,
           ssm_a_re, ssm_a_im, ssm_log_dt, ssm_b_re, ssm_b_im, ssm_c_re, ssm_c_im,
           ssm_d, w_glu, b_glu, g_pre_ffn, g_post_ffn, w_ffn_gate, w_ffn_up, w_ffn_down):
    batch, seq, d_model = x.shape
    depth = w_in.shape[0]
    a_heads = rel_bias.shape[1]
    b_heads = b_forget.shape[1]
    a_w, b_w = a_heads * HEAD_DIM, b_heads * HEAD_DIM
    ssm_w = ssm_d.shape[1]
    n_tok = batch * seq
    tok_cols = 3 * a_w + 3 * b_w
    assert w_in.shape[2] == tok_cols + b_heads + ssm_w
    assert b_heads <= LANES
    scale = HEAD_DIM ** -0.5

    w_in_t = jnp.swapaxes(w_in.astype(F32), 1, 2)
    col_scale = jnp.concatenate([
        jnp.full((a_w,), scale * LOG2E, F32), jnp.ones((2 * a_w,), F32),
        jnp.full((b_w,), scale * LOG2E, F32), jnp.ones((2 * b_w,), F32)])
    w_f = jnp.pad(w_in_t[:, tok_cols:tok_cols + b_heads, :],
                  ((0, 0), (0, LANES - b_heads), (0, 0))).astype(BF16)
    b_f = jnp.pad(b_forget.astype(F32), ((0, 0), (0, LANES - b_heads)))
    w_glu_t = jnp.swapaxes(w_glu, 1, 2).astype(BF16)
    w_out_b = w_out.astype(BF16)
    w_gate_b, w_up_b, w_down_b = (w.astype(BF16) for w in (w_ffn_gate, w_ffn_up, w_ffn_down))

    xf = x.reshape(n_tok, d_model).astype(F32)
    h = _rmsnorm(xf, g_pre_mix[0].astype(F32))
    for l in range(depth):
        proj = _inproj_tok(h, w_in_t, l, col_scale, tok_cols)
        u_t = _inproj_ssm(w_in_t, l, tok_cols, b_heads, ssm_w, h)
        k_aug = _forget_bias(h, w_f[l], b_f[l:l + 1], batch, seq, b_heads)

        o_a = _attention_a(proj, _attn_a_bias_vector(rel_bias[l]), batch, seq, a_heads,
                           0, a_heads, 2 * a_heads)
        c0 = 3 * a_heads
        o_b = _attention_b(proj, k_aug, batch, seq, b_heads,
                           c0, c0 + b_heads, c0 + 2 * b_heads)
        y_t = _ssm(u_t, ssm_d[l], ssm_a_re[l], ssm_a_im[l], ssm_log_dt[l],
                   ssm_b_re[l], ssm_b_im[l], ssm_c_re[l], ssm_c_im[l], seq)
        o_c_t = _glu(y_t, w_glu_t, l, b_glu[l])

        mix = _outproj(o_a, o_b, o_c_t, g_group[l], w_out_b, l)
        xf, h2 = _residual_norm(mix, xf, g_post_mix[l], g_pre_ffn[l])

        f = _ffn(h2, w_gate_b, w_up_b, w_down_b, l)
        g_next = g_pre_mix[l + 1] if l + 1 < depth else None
        xf, h = _residual_norm(f, xf, g_post_ffn[l], g_next)
    return xf.reshape(batch, seq, d_model).astype(x.dtype)
```
